```python
import math
import jax, jax.numpy as jnp
from jax import lax
import numpy as np

D_MODEL = 2048
BATCH = 4
SEQ = 4096
DEPTH = 2

CHUNK = 64
N_A = DEPTH // 2
N_B = DEPTH - N_A
GDN_HEADS = 16
GDN_DK = 128
GDN_DV = 128
GDN_CONV = 4
GDN_QK_DIM = GDN_HEADS * GDN_DK
GDN_V_DIM = GDN_HEADS * GDN_DV
GDN_IN_DIM = 2 * GDN_QK_DIM + 2 * GDN_V_DIM + 2 * GDN_HEADS
DIFF_HEADS = 8
DIFF_DH = 128
DIFF_QK_DIM = DIFF_HEADS * 2 * DIFF_DH
DIFF_V_DIM = DIFF_HEADS * 2 * DIFF_DH
Q_BLOCK = 128
D_FF = 5632
FFN_CONV = 3
EPS = 1e-6

kernel_name = 'yoco_gdn_diffattn_convffn_trunk'


def rmsnorm(x, w):
    xf = x.astype(jnp.float32)
    y = xf * lax.rsqrt(jnp.mean(xf * xf, axis=-1, keepdims=True) + EPS)
    return (y * w.astype(jnp.float32)).astype(x.dtype)


def causal_dwconv(x, w):
    K = w.shape[0]
    S = x.shape[1]
    xp = jnp.pad(x, ((0, 0), (K - 1, 0), (0, 0)))
    y = xp[:, 0:S] * w[0]
    for j in range(1, K):
        y = y + xp[:, j:j + S] * w[j]
    return y


def l2norm(x):
    return x * lax.rsqrt(jnp.sum(x * x, axis=-1, keepdims=True) + EPS)


def chunk_gated_delta_rule(q, k, v, g, beta):
    B, S, H, DK = q.shape
    DV = v.shape[-1]
    N = S // CHUNK

    def to_chunks(t):
        t = t.reshape((B, N, CHUNK, H) + t.shape[3:])
        return jnp.moveaxis(t, 3, 2)

    q = to_chunks(q * (DK ** -0.5))
    k = to_chunks(k)
    v = to_chunks(v)
    beta = to_chunks(beta)
    g = jnp.cumsum(to_chunks(g), axis=-1)
    idx = jnp.arange(CHUNK)
    causal = idx[:, None] >= idx[None, :]
    strict = idx[:, None] > idx[None, :]
    decay = jnp.exp(jnp.where(causal, g[..., :, None] - g[..., None, :], -jnp.inf))
    kb = k * beta[..., None]
    lower = jnp.where(strict, jnp.einsum('bnhid,bnhjd->bnhij', kb, k) * decay, 0.0)
    rhs = jnp.concatenate([v * beta[..., None], kb * jnp.exp(g)[..., None]], axis=-1)
    sol = lax.linalg.triangular_solve(lower, rhs, left_side=True, lower=True, unit_diagonal=True)
    u = sol[..., :DV]
    w = sol[..., DV:]
    a_intra = jnp.where(causal, jnp.einsum('bnhid,bnhjd->bnhij', q, k) * decay, 0.0)

    def step(state, xs):
        qc, kc, uc, wc, gc, ac = xs
        v_new = uc - jnp.einsum('bhck,bhkv->bhcv', wc, state)
        o = (jnp.einsum('bhck,bhkv->bhcv', qc * jnp.exp(gc)[..., None], state)
             + jnp.einsum('bhij,bhjv->bhiv', ac, v_new))
        g_last = gc[..., -1]
        k_dec = kc * jnp.exp(g_last[..., None] - gc)[..., None]
        state = state * jnp.exp(g_last)[..., None, None] + jnp.einsum('bhck,bhcv->bhkv', k_dec, v_new)
        return state, o

    xs = tuple(jnp.moveaxis(t, 1, 0) for t in (q, k, u, w, g, a_intra))
    state0 = jnp.zeros((B, H, DK, DV), jnp.float32)
    _, o = lax.scan(step, state0, xs)
    return jnp.transpose(o, (1, 0, 3, 2, 4)).reshape(B, S, H, DV)


def gdn_mixer(h, w_in, conv_w, a_log, dt_bias, norm_w, w_out):
    B, S, _ = h.shape
    proj = h @ w_in
    o_qkv = 2 * GDN_QK_DIM + GDN_V_DIM
    o_z = o_qkv + GDN_V_DIM
    qkv = jax.nn.silu(causal_dwconv(proj[..., :o_qkv], conv_w))
    z = proj[..., o_qkv:o_z]
    b = proj[..., o_z:o_z + GDN_HEADS]
    a = proj[..., o_z + GDN_HEADS:]
    q = l2norm(qkv[..., :GDN_QK_DIM].astype(jnp.float32).reshape(B, S, GDN_HEADS, GDN_DK))
    k = l2norm(qkv[..., GDN_QK_DIM:2 * GDN_QK_DIM].astype(jnp.float32).reshape(B, S, GDN_HEADS, GDN_DK))
    v = qkv[..., 2 * GDN_QK_DIM:].astype(jnp.float32).reshape(B, S, GDN_HEADS, GDN_DV)
    beta = jax.nn.sigmoid(b.astype(jnp.float32))
    g = -jnp.exp(a_log.astype(jnp.float32)) * jax.nn.softplus(a.astype(jnp.float32) + dt_bias.astype(jnp.float32))
    o = chunk_gated_delta_rule(q, k, v, g, beta)
    o = rmsnorm(o, norm_w) * jax.nn.silu(z.astype(jnp.float32).reshape(B, S, GDN_HEADS, GDN_DV))
    return o.reshape(B, S, GDN_V_DIM).astype(h.dtype) @ w_out


def diff_attention(h, w_q, k_sh, v_sh, lq1, lk1, lq2, lk2, subln_w, w_o, lambda_init):
    B, S, _ = h.shape
    q = (h @ w_q).reshape(B, S, DIFF_HEADS, 2, DIFF_DH)
    lam = (jnp.exp(jnp.sum(lq1.astype(jnp.float32) * lk1.astype(jnp.float32)))
           - jnp.exp(jnp.sum(lq2.astype(jnp.float32) * lk2.astype(jnp.float32))) + lambda_init)
    scale = DIFF_DH ** -0.5
    chunk_id = jnp.arange(S) // CHUNK
    outs = []
    for q0 in range(0, S, Q_BLOCK):
        k_end = q0 + Q_BLOCK
        s = jnp.einsum('bqhmd,bkhmd->bhmqk', q[:, q0:k_end], k_sh[:, :k_end]).astype(jnp.float32) * scale
        mask = chunk_id[None, :k_end] <= chunk_id[q0:k_end, None]
        p = jax.nn.softmax(jnp.where(mask, s, -jnp.inf), axis=-1)
        a = p[:, :, 0] - lam * p[:, :, 1]
        outs.append(jnp.einsum('bhqk,bkhe->bqhe', a.astype(v_sh.dtype), v_sh[:, :k_end]))
    o = jnp.concatenate(outs, axis=1)
    o = rmsnorm(o, subln_w) * (1.0 - lambda_init)
    return o.reshape(B, S, DIFF_V_DIM) @ w_o


def conv_ffn(h, w_up, conv_w, conv_b, w_down):
    gu = h @ w_up
    gate = causal_dwconv(gu[..., :D_FF], conv_w) + conv_b
    return (jax.nn.silu(gate) * gu[..., D_FF:]) @ w_down


def setup_inputs(seed: int = 0) -> dict:
    key = jax.random.key(seed)
    ks = jax.random.split(key, 24)
    f32 = jnp.float32

    def dense(k, shape, fan_in):
        return jax.random.normal(k, shape, f32) * (fan_in ** -0.5)

    def gain(k, shape):
        return 1.0 + 0.02 * jax.random.normal(k, shape, f32)

    dt = jnp.exp(jax.random.uniform(ks[9], (N_A, GDN_HEADS), f32, math.log(1e-3), math.log(1e-1)))
    return {
        'x': jax.random.normal(ks[0], (BATCH, SEQ, D_MODEL), f32),
        'norm_mix_pre': gain(ks[1], (DEPTH, D_MODEL)),
        'norm_mix_post': gain(ks[2], (DEPTH, D_MODEL)),
        'norm_ffn_pre': gain(ks[3], (DEPTH, D_MODEL)),
        'norm_ffn_post': gain(ks[4], (DEPTH, D_MODEL)),
        'gdn_w_in': dense(ks[5], (N_A, D_MODEL, GDN_IN_DIM), D_MODEL),
        'gdn_conv_w': dense(ks[6], (N_A, GDN_CONV, 2 * GDN_QK_DIM + GDN_V_DIM), GDN_CONV),
        'gdn_a_log': jnp.log(jax.random.uniform(ks[7], (N_A, GDN_HEADS), f32, 1.0, 16.0)),
        'gdn_dt_bias': dt + jnp.log(-jnp.expm1(-dt)),
        'gdn_norm_w': gain(ks[8], (N_A, GDN_DV)),
        'gdn_w_out': dense(ks[10], (N_A, GDN_V_DIM, D_MODEL), GDN_V_DIM),
        'kv_norm_w': gain(ks[11], (D_MODEL,)),
        'w_kv': dense(ks[12], (D_MODEL, DIFF_QK_DIM + DIFF_V_DIM), D_MODEL),
        'diff_w_q': dense(ks[13], (N_B, D_MODEL, DIFF_QK_DIM), D_MODEL),
        'diff_lq1': 0.1 * jax.random.normal(ks[14], (N_B, DIFF_DH), f32),
        'diff_lk1': 0.1 * jax.random.normal(ks[15], (N_B, DIFF_DH), f32),
        'diff_lq2': 0.1 * jax.random.normal(ks[16], (N_B, DIFF_DH), f32),
        'diff_lk2': 0.1 * jax.random.normal(ks[17], (N_B, DIFF_DH), f32),
        'diff_subln_w': gain(ks[18], (N_B, 2 * DIFF_DH)),
        'diff_w_o': dense(ks[19], (N_B, DIFF_V_DIM, D_MODEL), DIFF_V_DIM),
        'ffn_w_up': dense(ks[20], (DEPTH, D_MODEL, 2 * D_FF), D_MODEL),
        'ffn_conv_w': dense(ks[21], (DEPTH, FFN_CONV, D_FF), FFN_CONV),
        'ffn_conv_b': 0.01 * jax.random.normal(ks[22], (DEPTH, D_FF), f32),
        'ffn_w_down': dense(ks[23], (DEPTH, D_FF, D_MODEL), D_FF),
    }


def reference(x, norm_mix_pre, norm_mix_post, norm_ffn_pre, norm_ffn_post,
              gdn_w_in, gdn_conv_w, gdn_a_log, gdn_dt_bias, gdn_norm_w, gdn_w_out,
              kv_norm_w, w_kv, diff_w_q, diff_lq1, diff_lk1, diff_lq2, diff_lk2,
              diff_subln_w, diff_w_o, ffn_w_up, ffn_conv_w, ffn_conv_b, ffn_w_down):
    B, S, _ = x.shape
    h = x
    k_sh = None
    v_sh = None
    for layer in range(DEPTH):
        if layer == N_A:
            kv = rmsnorm(h, kv_norm_w) @ w_kv
            k_sh = kv[..., :DIFF_QK_DIM].reshape(B, S, DIFF_HEADS, 2, DIFF_DH)
            v_sh = kv[..., DIFF_QK_DIM:].reshape(B, S, DIFF_HEADS, 2 * DIFF_DH)
        hn = rmsnorm(h, norm_mix_pre[layer])
        if layer < N_A:
            m = gdn_mixer(hn, gdn_w_in[layer], gdn_conv_w[layer], gdn_a_log[layer],
                          gdn_dt_bias[layer], gdn_norm_w[layer], gdn_w_out[layer])
        else:
            j = layer - N_A
            lambda_init = 0.8 - 0.6 * math.exp(-0.3 * layer)
            m = diff_attention(hn, diff_w_q[j], k_sh, v_sh, diff_lq1[j], diff_lk1[j],
                               diff_lq2[j], diff_lk2[j], diff_subln_w[j], diff_w_o[j], lambda_init)
        h = h + rmsnorm(m, norm_mix_post[layer])
        f = conv_ffn(rmsnorm(h, norm_ffn_pre[layer]), ffn_w_up[layer], ffn_conv_w[layer],
                     ffn_conv_b[layer], ffn_w_down[layer])
        h = h + rmsnorm(f, norm_ffn_post[layer])
    return h
```

```python
import functools
import math

import jax
import jax.numpy as jnp
from jax import lax
from jax.experimental import pallas as pl
from jax.experimental.pallas import tpu as pltpu

D_MODEL = 2048
DEPTH = 2
N_A = DEPTH // 2
CHUNK = 64
GDN_HEADS = 16
GDN_DK = 128
GDN_DV = 128
GDN_CONV = 4
GDN_QK_DIM = GDN_HEADS * GDN_DK
GDN_V_DIM = GDN_HEADS * GDN_DV
DIFF_HEADS = 8
DIFF_DH = 128
DIFF_QK_DIM = DIFF_HEADS * 2 * DIFF_DH
D_FF = 5632
FFN_CONV = 3
EPS = 1e-6

LANES = 128
SUBLANES = 8
VMEM_LIMIT_BYTES = 56 * 1024 * 1024

DELTA_CHUNK = 128
INV_BASE = 16

F32 = jnp.float32
BF16 = jnp.bfloat16
NT_DIMS = (((1,), (1,)), ((), ()))
TN_DIMS = (((0,), (0,)), ((), ()))


def _bdot(a, b):
    return jnp.dot(a.astype(BF16), b.astype(BF16), preferred_element_type=F32)


def _rms_scale(x, gain):
    ms = jnp.mean(x * x, axis=-1, keepdims=True)
    return x * lax.rsqrt(ms + EPS) * gain


def _silu(x):
    return x * jax.nn.sigmoid(x)


def _params(*sem):
    return pltpu.CompilerParams(dimension_semantics=sem, vmem_limit_bytes=VMEM_LIMIT_BYTES)


def _norm_matmul_kernel(x_ref, g_ref, w_ref, o_ref, xn_ref):
    @pl.when(pl.program_id(1) == 0)
    def _():
        xn_ref[...] = _rms_scale(x_ref[...], g_ref[...]).astype(BF16)

    o_ref[...] = jnp.dot(xn_ref[...], w_ref[...], preferred_element_type=F32).astype(o_ref.dtype)


def norm_matmul(x, gain, w, out_dtype, *, tm, tn):
    m, k = x.shape
    n = w.shape[1]
    return pl.pallas_call(
        _norm_matmul_kernel,
        grid=(m // tm, n // tn),
        in_specs=[
            pl.BlockSpec((tm, k), lambda i, j: (i, 0)),
            pl.BlockSpec((1, k), lambda i, j: (0, 0)),
            pl.BlockSpec((k, tn), lambda i, j: (0, j)),
        ],
        out_specs=pl.BlockSpec((tm, tn), lambda i, j: (i, j)),
        out_shape=jax.ShapeDtypeStruct((m, n), out_dtype),
        scratch_shapes=[pltpu.VMEM((tm, k), BF16)],
        compiler_params=_params("parallel", "arbitrary"),
        name="norm_matmul",
    )(x, gain.reshape(1, k), w)


def _matmul_norm_res_kernel(a_ref, w_ref, h_ref, g_ref, o_ref):
    m = jnp.dot(a_ref[...], w_ref[...], preferred_element_type=F32)
    o_ref[...] = h_ref[...] + _rms_scale(m, g_ref[...])


def matmul_norm_res(a, w, h, gain, *, tm):
    m, k = a.shape
    n = w.shape[1]
    return pl.pallas_call(
        _matmul_norm_res_kernel,
        grid=(m // tm,),
        in_specs=[
            pl.BlockSpec((tm, k), lambda i: (i, 0)),
            pl.BlockSpec((k, n), lambda i: (0, 0)),
            pl.BlockSpec((tm, n), lambda i: (i, 0)),
            pl.BlockSpec((1, n), lambda i: (0, 0)),
        ],
        out_specs=pl.BlockSpec((tm, n), lambda i: (i, 0)),
        out_shape=jax.ShapeDtypeStruct((m, n), F32),
        compiler_params=_params("parallel"),
        name="matmul_norm_res",
    )(a, w, h, gain.reshape(1, n))


def _ffn_kernel(h_ref, gpre_ref, wg_ref, wu_ref, cw_ref, cb_ref, wd_ref, gpost_ref, o_ref,
                xn_ref, acc_ref, gpad_ref, carry_ref, *, tm, tiles_per_seq):
    i = pl.program_id(0)
    f = pl.program_id(1)

    @pl.when(f == 0)
    def _():
        xn_ref[...] = _rms_scale(h_ref[...], gpre_ref[...]).astype(BF16)
        acc_ref[...] = jnp.zeros_like(acc_ref)

    xn = xn_ref[...]
    gate = jnp.dot(xn, wg_ref[...], preferred_element_type=F32)
    up = jnp.dot(xn, wu_ref[...], preferred_element_type=F32)

    seq_start = (i % tiles_per_seq) == 0
    prev = carry_ref[f]
    gpad_ref[0:SUBLANES, :] = jnp.where(seq_start, 0.0, prev)
    gpad_ref[SUBLANES:, :] = gate
    carry_ref[f] = gate[tm - SUBLANES:, :]
    g1 = gpad_ref[SUBLANES - 1:SUBLANES - 1 + tm, :]
    g2 = gpad_ref[SUBLANES - 2:SUBLANES - 2 + tm, :]
    conv = g2 * cw_ref[0:1, :] + g1 * cw_ref[1:2, :] + gate * cw_ref[2:3, :] + cb_ref[...]
    act = (_silu(conv) * up).astype(BF16)
    acc_ref[...] += jnp.dot(act, wd_ref[...], preferred_element_type=F32)

    @pl.when(f == pl.num_programs(1) - 1)
    def _():
        o_ref[...] = h_ref[...] + _rms_scale(acc_ref[...], gpost_ref[...])


def conv_ffn(h, gpre, w_up, conv_w, conv_b, w_down, gpost, *, seq, tm, tf):
    m, d = h.shape
    dff = w_down.shape[0]
    nf = dff // tf
    kern = functools.partial(_ffn_kernel, tm=tm, tiles_per_seq=seq // tm)
    return pl.pallas_call(
        kern,
        grid=(m // tm, nf),
        in_specs=[
            pl.BlockSpec((tm, d), lambda i, f: (i, 0)),
            pl.BlockSpec((1, d), lambda i, f: (0, 0)),
            pl.BlockSpec((d, tf), lambda i, f: (0, f)),
            pl.BlockSpec((d, tf), lambda i, f: (0, f + nf)),
            pl.BlockSpec((FFN_CONV, tf), lambda i, f: (0, f)),
            pl.BlockSpec((1, tf), lambda i, f: (0, f)),
            pl.BlockSpec((tf, d), lambda i, f: (f, 0)),
            pl.BlockSpec((1, d), lambda i, f: (0, 0)),
        ],
        out_specs=pl.BlockSpec((tm, d), lambda i, f: (i, 0)),
        out_shape=jax.ShapeDtypeStruct((m, d), F32),
        scratch_shapes=[
            pltpu.VMEM((tm, d), BF16),
            pltpu.VMEM((tm, d), F32),
            pltpu.VMEM((tm + SUBLANES, tf), F32),
            pltpu.VMEM((nf, SUBLANES, tf), F32),
        ],
        compiler_params=_params("arbitrary", "arbitrary"),
        name="conv_ffn",
    )(h, gpre.reshape(1, d), w_up, w_up, conv_w, conv_b.reshape(1, dff), w_down, gpost.reshape(1, d))


def _gates_kernel(ba_ref, prm_ref, g_ref, gt_ref):
    ba = ba_ref[...]
    lane = lax.broadcasted_iota(jnp.int32, ba.shape, 1)
    is_beta = lane < GDN_HEADS
    is_g = jnp.logical_and(lane >= GDN_HEADS, lane < 2 * GDN_HEADS)
    g = -jnp.exp(prm_ref[0:1, :]) * jax.nn.softplus(ba + prm_ref[1:2, :])
    g = jnp.where(is_g, g, 0.0)
    c = ba.shape[0]
    row = lax.broadcasted_iota(jnp.int32, (c, c), 0)
    col = lax.broadcasted_iota(jnp.int32, (c, c), 1)
    tril = (row >= col).astype(F32)
    gcum = jnp.dot(tril, g, preferred_element_type=F32, precision=lax.Precision.HIGHEST)
    out = jnp.where(is_beta, jax.nn.sigmoid(ba), gcum)
    g_ref[...] = out
    gt_ref[0] = out.T


def gdn_gates(ba, prm):
    t = ba.shape[0]
    c = DELTA_CHUNK
    return pl.pallas_call(
        _gates_kernel,
        grid=(t // c,),
        in_specs=[
            pl.BlockSpec((c, LANES), lambda i: (i, 0)),
            pl.BlockSpec((SUBLANES, LANES), lambda i: (0, 0)),
        ],
        out_specs=[
            pl.BlockSpec((c, LANES), lambda i: (i, 0)),
            pl.BlockSpec((1, LANES, c), lambda i: (i, 0, 0)),
        ],
        out_shape=[
            jax.ShapeDtypeStruct((t, LANES), F32),
            jax.ShapeDtypeStruct((t // c, LANES, c), F32),
        ],
        compiler_params=_params("parallel"),
        name="gdn_gates",
    )(ba, prm)


def _unit_lower_inverse(low, masks):
    eye, base_mask, merge_masks = masks
    a = jnp.where(base_mask, -low, 0.0)
    x = eye + a
    span = 2
    while span < INV_BASE:
        a = _bdot(a, a)
        x = x + _bdot(x, a)
        span *= 2
    for mm in merge_masks:
        m = jnp.where(mm, low, 0.0)
        x = x - _bdot(_bdot(x, m), x)
    return x


def _inverse_masks(c):
    row = lax.broadcasted_iota(jnp.int32, (c, c), 0)
    col = lax.broadcasted_iota(jnp.int32, (c, c), 1)
    eye = (row == col).astype(F32)
    same = lambda b: (row // b) == (col // b)
    base_mask = same(INV_BASE)
    merge_masks = []
    b = INV_BASE
    while b < c:
        merge_masks.append(jnp.logical_and(same(2 * b), jnp.logical_not(same(b))))
        b *= 2
    return eye, base_mask, merge_masks


def _gdn_kernel(q_ref, k_ref, v_ref, z_ref, g_ref, gt_ref, cw_ref, nw_ref, o_ref,
                xpad_ref, state_ref, u_ref, wq_ref, kd_ref, ai_ref, eg_ref, *, hb, ts):
    c = DELTA_CHUNK
    nchunk = ts // c
    hg = pl.program_id(1)
    tb = pl.program_id(2)
    halo = SUBLANES

    @pl.when(tb == 0)
    def _():
        xpad_ref[:, 0:halo, :] = jnp.zeros((3, halo, hb * LANES), F32)
        state_ref[...] = jnp.zeros_like(state_ref)

    @pl.when(tb > 0)
    def _():
        xpad_ref[:, 0:halo, :] = xpad_ref[:, ts:ts + halo, :]

    xpad_ref[0, halo:, :] = q_ref[...].astype(F32)
    xpad_ref[1, halo:, :] = k_ref[...].astype(F32)
    xpad_ref[2, halo:, :] = v_ref[...].astype(F32)

    row = lax.broadcasted_iota(jnp.int32, (c, c), 0)
    col = lax.broadcasted_iota(jnp.int32, (c, c), 1)
    causal = row >= col
    strict = row > col
    inv_masks = _inverse_masks(c)
    lane = lax.broadcasted_iota(jnp.int32, (c, LANES), 1)
    q_scale = GDN_DK ** -0.5

    def conv_silu(t, r0, i):
        lanes = slice(i * LANES, (i + 1) * LANES)
        win = xpad_ref[t, pl.ds(r0, c + halo), lanes]
        acc = None
        for j in range(GDN_CONV):
            off = halo - (GDN_CONV - 1) + j
            term = win[off:off + c, :] * cw_ref[j, t:t + 1, lanes]
            acc = term if acc is None else acc + term
        return _silu(acc)

    def l2norm(x):
        return x * lax.rsqrt(jnp.sum(x * x, axis=-1, keepdims=True) + EPS)

    def phase_a(ci, carry):
        r0 = pl.multiple_of(ci * c, c)
        gates = g_ref[pl.ds(r0, c), :]
        for i in range(hb):
            head = hg * hb + i
            lanes = slice(i * LANES, (i + 1) * LANES)
            beta = jnp.sum(jnp.where(lane == head, gates, 0.0), axis=-1, keepdims=True)
            gc = jnp.sum(jnp.where(lane == head + GDN_HEADS, gates, 0.0), axis=-1, keepdims=True)
            gc_row = gt_ref[ci, pl.ds(head + GDN_HEADS, 1), :]
            g_last = gc_row[:, c - 1:c]
            qn = l2norm(conv_silu(0, r0, i)) * q_scale
            kn = l2norm(conv_silu(1, r0, i))
            vv = conv_silu(2, r0, i)
            kb = kn * beta
            qk = lax.dot_general(jnp.concatenate([qn, kb], axis=0).astype(BF16), kn.astype(BF16),
                                 NT_DIMS, preferred_element_type=F32)
            decay = jnp.exp(jnp.where(causal, gc - gc_row, -jnp.inf))
            a_intra = jnp.where(causal, qk[:c] * decay, 0.0)
            low = jnp.where(strict, qk[c:] * decay, 0.0)
            tinv = _unit_lower_inverse(low, inv_masks)
            egc = jnp.exp(gc)
            rhs = jnp.concatenate([vv * beta, kb * egc], axis=1)
            sol = _bdot(tinv, rhs)
            u_ref[pl.ds(r0, c), lanes] = sol[:, :LANES]
            wq_ref[0, pl.ds(r0, c), lanes] = sol[:, LANES:].astype(BF16)
            wq_ref[1, pl.ds(r0, c), lanes] = (qn * egc).astype(BF16)
            kd_ref[pl.ds(r0, c), lanes] = (kn * jnp.exp(g_last - gc)).astype(BF16)
            ai_ref[pl.ds(r0, c), lanes] = a_intra.astype(BF16)
            eg_ref[ci, i] = jnp.broadcast_to(jnp.exp(g_last), (SUBLANES, LANES))
        return carry

    lax.fori_loop(0, nchunk, phase_a, 0)

    def phase_b(ci, carry):
        r0 = pl.multiple_of(ci * c, c)
        for i in range(hb):
            lanes = slice(i * LANES, (i + 1) * LANES)
            state = state_ref[i]
            wq = jnp.concatenate([wq_ref[0, pl.ds(r0, c), lanes], wq_ref[1, pl.ds(r0, c), lanes]], axis=0)
            ws = jnp.dot(wq, state.astype(BF16), preferred_element_type=F32)
            v_new = u_ref[pl.ds(r0, c), lanes] - ws[:c]
            v_new_b = v_new.astype(BF16)
            o = ws[c:] + jnp.dot(ai_ref[pl.ds(r0, c), lanes], v_new_b, preferred_element_type=F32)
            upd = lax.dot_general(kd_ref[pl.ds(r0, c), lanes], v_new_b, TN_DIMS, preferred_element_type=F32)
            state_ref[i] = state * eg_ref[ci, i][0:1, 0:1] + upd
            z = z_ref[pl.ds(r0, c), lanes].astype(F32)
            o_ref[pl.ds(r0, c), lanes] = (_rms_scale(o, nw_ref[...]) * _silu(z)).astype(o_ref.dtype)
        return carry

    lax.fori_loop(0, nchunk, phase_b, 0)


def gdn_core(proj, gates, gates_t, conv_w, norm_w, *, batch, seq, hb, ts):
    t = proj.shape[0]
    w = hb * LANES
    ngroups = GDN_HEADS // hb
    nt = seq // ts
    nchunk = ts // DELTA_CHUNK
    kern = functools.partial(_gdn_kernel, hb=hb, ts=ts)

    def col_spec(base):
        return pl.BlockSpec((ts, w), lambda b, h, s: (b * nt + s, base * ngroups + h))

    return pl.pallas_call(
        kern,
        grid=(batch, ngroups, nt),
        in_specs=[
            col_spec(0), col_spec(1), col_spec(2), col_spec(3),
            pl.BlockSpec((ts, LANES), lambda b, h, s: (b * nt + s, 0)),
            pl.BlockSpec((nchunk, LANES, DELTA_CHUNK), lambda b, h, s: (b * nt + s, 0, 0)),
            pl.BlockSpec((GDN_CONV, 3, w), lambda b, h, s: (0, 0, h)),
            pl.BlockSpec((1, LANES), lambda b, h, s: (0, 0)),
        ],
        out_specs=pl.BlockSpec((ts, w), lambda b, h, s: (b * nt + s, h)),
        out_shape=jax.ShapeDtypeStruct((t, GDN_V_DIM), BF16),
        scratch_shapes=[
            pltpu.VMEM((3, ts + SUBLANES, w), F32),
            pltpu.VMEM((hb, GDN_DK, GDN_DV), F32),
            pltpu.VMEM((ts, w), F32),
            pltpu.VMEM((2, ts, w), BF16),
            pltpu.VMEM((ts, w), BF16),
            pltpu.VMEM((ts, w), BF16),
            pltpu.VMEM((nchunk, hb, SUBLANES, LANES), F32),
        ],
        compiler_params=_params("parallel", "parallel", "arbitrary"),
        name="gdn_core",
    )(proj, proj, proj, proj, gates, gates_t, conv_w, norm_w.reshape(1, LANES))


def _attn_kernel(qi_tab, ki_tab, q_ref, k_ref, v_ref, lam_ref, sw_ref, o_ref,
                 m_ref, l_ref, acc_ref, *, tq, tk, lambda_init):
    p = pl.program_id(2)
    qi = qi_tab[p]
    ki = ki_tab[p]
    dh = DIFF_DH

    @pl.when(ki == 0)
    def _():
        m_ref[...] = jnp.full_like(m_ref, -jnp.inf)
        l_ref[...] = jnp.zeros_like(l_ref)
        acc_ref[...] = jnp.zeros_like(acc_ref)

    q = q_ref[...]
    k = k_ref[...]
    v = v_ref[...]
    q_chunk = (qi * tq + lax.broadcasted_iota(jnp.int32, (tq, tk), 0)) // CHUNK
    k_chunk = (ki * tk + lax.broadcasted_iota(jnp.int32, (tq, tk), 1)) // CHUNK
    visible = k_chunk <= q_chunk
    scale = dh ** -0.5
    for m in range(2):
        s = lax.dot_general(q[:, m * dh:(m + 1) * dh], k[:, m * dh:(m + 1) * dh], NT_DIMS,
                            preferred_element_type=F32) * scale
        s = jnp.where(visible, s, -jnp.inf)
        m_old = m_ref[m]
        m_new = jnp.maximum(m_old, jnp.max(s, axis=-1, keepdims=True))
        alpha = jnp.exp(m_old - m_new)
        pr = jnp.exp(s - m_new)
        l_ref[m] = alpha * l_ref[m] + jnp.sum(pr, axis=-1, keepdims=True)
        acc_ref[m] = alpha * acc_ref[m] + jnp.dot(pr.astype(BF16), v, preferred_element_type=F32)
        m_ref[m] = m_new

    @pl.when((ki + 1) * tk >= (qi + 1) * tq)
    def _():
        lam = (jnp.exp(jnp.sum(lam_ref[0:1, :] * lam_ref[1:2, :], axis=-1, keepdims=True))
               - jnp.exp(jnp.sum(lam_ref[2:3, :] * lam_ref[3:4, :], axis=-1, keepdims=True))
               + lambda_init)
        a = acc_ref[0] / l_ref[0] - lam * (acc_ref[1] / l_ref[1])
        o_ref[...] = (_rms_scale(a, sw_ref[...]) * (1.0 - lambda_init)).astype(o_ref.dtype)


def diff_attention(q, kv, lam_rows, subln_w, *, batch, seq, tq, tk, lambda_init):
    t = q.shape[0]
    hw = 2 * DIFF_DH
    nq = seq // tq
    nk = seq // tk
    pairs = [(a, b) for a in range(nq) for b in range(nk) if b * tk < (a + 1) * tq]
    qi_tab = jnp.asarray([a for a, _ in pairs], jnp.int32)
    ki_tab = jnp.asarray([b for _, b in pairs], jnp.int32)
    kern = functools.partial(_attn_kernel, tq=tq, tk=tk, lambda_init=lambda_init)
    grid_spec = pltpu.PrefetchScalarGridSpec(
        num_scalar_prefetch=2,
        grid=(batch, DIFF_HEADS, len(pairs)),
        in_specs=[
            pl.BlockSpec((tq, hw), lambda b, h, p, qt, kt: (b * nq + qt[p], h)),
            pl.BlockSpec((tk, hw), lambda b, h, p, qt, kt: (b * nk + kt[p], h)),
            pl.BlockSpec((tk, hw), lambda b, h, p, qt, kt: (b * nk + kt[p], DIFF_HEADS + h)),
            pl.BlockSpec((SUBLANES, DIFF_DH), lambda b, h, p, qt, kt: (0, 0)),
            pl.BlockSpec((1, hw), lambda b, h, p, qt, kt: (0, 0)),
        ],
        out_specs=pl.BlockSpec((tq, hw), lambda b, h, p, qt, kt: (b * nq + qt[p], h)),
        scratch_shapes=[
            pltpu.VMEM((2, tq, 1), F32),
            pltpu.VMEM((2, tq, 1), F32),
            pltpu.VMEM((2, tq, hw), F32),
        ],
    )
    return pl.pallas_call(
        kern,
        grid_spec=grid_spec,
        out_shape=jax.ShapeDtypeStruct((t, DIFF_HEADS * hw), BF16),
        compiler_params=_params("parallel", "parallel", "arbitrary"),
        name="diff_attention",
    )(qi_tab, ki_tab, q, kv, kv, lam_rows, subln_w.reshape(1, hw))


def kernel(x, norm_mix_pre, norm_mix_post, norm_ffn_pre, norm_ffn_post, gdn_w_in, gdn_conv_w, gdn_a_log,
           gdn_dt_bias, gdn_norm_w, gdn_w_out, kv_norm_w, w_kv, diff_w_q, diff_lq1, diff_lk1, diff_lq2,
           diff_lk2, diff_subln_w, diff_w_o, ffn_w_up, ffn_conv_w, ffn_conv_b, ffn_w_down):
    batch, seq, d = x.shape
    t = batch * seq
    h = x.reshape(t, d)
    n_main = 2 * GDN_QK_DIM + 2 * GDN_V_DIM

    def ffn(h, layer):
        return conv_ffn(h, norm_ffn_pre[layer], ffn_w_up[layer].astype(BF16), ffn_conv_w[layer],
                        ffn_conv_b[layer], ffn_w_down[layer].astype(BF16), norm_ffn_post[layer],
                        seq=seq, tm=512, tf=512)

    for layer in range(DEPTH):
        if layer < N_A:
            w_in = gdn_w_in[layer]
            proj = norm_matmul(h, norm_mix_pre[layer], w_in[:, :n_main].astype(BF16), BF16, tm=1024, tn=512)
            w_ba = jnp.pad(w_in[:, n_main:], ((0, 0), (0, LANES - 2 * GDN_HEADS))).astype(BF16)
            ba = norm_matmul(h, norm_mix_pre[layer], w_ba, F32, tm=1024, tn=LANES)
            prm = jnp.zeros((SUBLANES, LANES), F32)
            prm = prm.at[0, GDN_HEADS:2 * GDN_HEADS].set(gdn_a_log[layer].astype(F32))
            prm = prm.at[1, GDN_HEADS:2 * GDN_HEADS].set(gdn_dt_bias[layer].astype(F32))
            gates, gates_t = gdn_gates(ba, prm)
            conv_w = gdn_conv_w[layer].reshape(GDN_CONV, 3, GDN_QK_DIM)
            o = gdn_core(proj, gates, gates_t, conv_w, gdn_norm_w[layer], batch=batch, seq=seq, hb=2, ts=1024)
            h = matmul_norm_res(o, gdn_w_out[layer].astype(BF16), h, norm_mix_post[layer], tm=512)
        else:
            j = layer - N_A
            if layer == N_A:
                kv = norm_matmul(h, kv_norm_w, w_kv.astype(BF16), BF16, tm=1024, tn=512)
            lambda_init = 0.8 - 0.6 * math.exp(-0.3 * layer)
            q = norm_matmul(h, norm_mix_pre[layer], diff_w_q[j].astype(BF16), BF16, tm=1024, tn=512)
            lam_rows = jnp.zeros((SUBLANES, DIFF_DH), F32)
            lam_rows = lam_rows.at[0].set(diff_lq1[j]).at[1].set(diff_lk1[j])
            lam_rows = lam_rows.at[2].set(diff_lq2[j]).at[3].set(diff_lk2[j])
            att = diff_attention(q, kv, lam_rows, diff_subln_w[j], batch=batch, seq=seq, tq=512, tk=512,
                                 lambda_init=lambda_init)
            h = matmul_norm_res(att, diff_w_o[j].astype(BF16), h, norm_mix_post[layer], tm=512)
        h = ffn(h, layer)
    return h.reshape(batch, seq, d)
```

```python
import functools
import math

import jax
import jax.numpy as jnp
from jax import lax
from jax.experimental import pallas as pl
from jax.experimental.pallas import tpu as pltpu

D_MODEL = 2048
DEPTH = 2
N_A = DEPTH // 2
CHUNK = 64
GDN_HEADS = 16
GDN_DK = 128
GDN_DV = 128
GDN_CONV = 4
GDN_QK_DIM = GDN_HEADS * GDN_DK
GDN_V_DIM = GDN_HEADS * GDN_DV
DIFF_HEADS = 8
DIFF_DH = 128
DIFF_QK_DIM = DIFF_HEADS * 2 * DIFF_DH
D_FF = 5632
FFN_CONV = 3
EPS = 1e-6

LANES = 128
SUBLANES = 8
VMEM_LIMIT_BYTES = 56 * 1024 * 1024

DELTA_CHUNK = 128
INV_BASE = 16

F32 = jnp.float32
BF16 = jnp.bfloat16
NT_DIMS = (((1,), (1,)), ((), ()))
TN_DIMS = (((0,), (0,)), ((), ()))


def _bdot(a, b):
    return jnp.dot(a.astype(BF16), b.astype(BF16), preferred_element_type=F32)


def _rms_scale(x, gain):
    ms = jnp.mean(x * x, axis=-1, keepdims=True)
    return x * lax.rsqrt(ms + EPS) * gain


def _silu(x):
    return x * jax.nn.sigmoid(x)


def _params(*sem):
    return pltpu.CompilerParams(dimension_semantics=sem, vmem_limit_bytes=VMEM_LIMIT_BYTES)


def _norm_matmul_kernel(x_ref, g_ref, w_ref, o_ref, xn_ref, *, out_scale):
    @pl.when(pl.program_id(1) == 0)
    def _():
        xn_ref[...] = _rms_scale(x_ref[...], g_ref[...]).astype(BF16)

    y = jnp.dot(xn_ref[...], w_ref[...], preferred_element_type=F32)
    if out_scale is not None:
        y = y * out_scale
    o_ref[...] = y.astype(o_ref.dtype)


def norm_matmul(x, gain, w, out_dtype, *, tm, tn, out_scale=None):
    m, k = x.shape
    n = w.shape[1]
    return pl.pallas_call(
        functools.partial(_norm_matmul_kernel, out_scale=out_scale),
        grid=(m // tm, n // tn),
        in_specs=[
            pl.BlockSpec((tm, k), lambda i, j: (i, 0)),
            pl.BlockSpec((1, k), lambda i, j: (0, 0)),
            pl.BlockSpec((k, tn), lambda i, j: (0, j)),
        ],
        out_specs=pl.BlockSpec((tm, tn), lambda i, j: (i, j)),
        out_shape=jax.ShapeDtypeStruct((m, n), out_dtype),
        scratch_shapes=[pltpu.VMEM((tm, k), BF16)],
        compiler_params=_params("parallel", "arbitrary"),
        name="norm_matmul",
    )(x, gain.reshape(1, k), w)


def _matmul_norm_res_kernel(a_ref, w_ref, h_ref, g_ref, o_ref):
    m = jnp.dot(a_ref[...], w_ref[...], preferred_element_type=F32)
    o_ref[...] = h_ref[...] + _rms_scale(m, g_ref[...])


def matmul_norm_res(a, w, h, gain, *, tm):
    m, k = a.shape
    n = w.shape[1]
    return pl.pallas_call(
        _matmul_norm_res_kernel,
        grid=(m // tm,),
        in_specs=[
            pl.BlockSpec((tm, k), lambda i: (i, 0)),
            pl.BlockSpec((k, n), lambda i: (0, 0)),
            pl.BlockSpec((tm, n), lambda i: (i, 0)),
            pl.BlockSpec((1, n), lambda i: (0, 0)),
        ],
        out_specs=pl.BlockSpec((tm, n), lambda i: (i, 0)),
        out_shape=jax.ShapeDtypeStruct((m, n), F32),
        compiler_params=_params("parallel"),
        name="matmul_norm_res",
    )(a, w, h, gain.reshape(1, n))


def _ffn_kernel(h_ref, gpre_ref, wg_ref, wu_ref, cw_ref, cb_ref, wd_ref, gpost_ref, o_ref,
                xn_ref, acc_ref, gpad_ref, carry_ref, *, tm, tiles_per_seq):
    i = pl.program_id(0)
    f = pl.program_id(1)

    @pl.when(f == 0)
    def _():
        xn_ref[...] = _rms_scale(h_ref[...], gpre_ref[...]).astype(BF16)
        acc_ref[...] = jnp.zeros_like(acc_ref)

    xn = xn_ref[...]
    gate = jnp.dot(xn, wg_ref[...], preferred_element_type=F32)
    up = jnp.dot(xn, wu_ref[...], preferred_element_type=F32)

    seq_start = (i % tiles_per_seq) == 0
    prev = carry_ref[f]
    gpad_ref[0:SUBLANES, :] = jnp.where(seq_start, 0.0, prev)
    gpad_ref[SUBLANES:, :] = gate
    carry_ref[f] = gate[tm - SUBLANES:, :]
    g1 = gpad_ref[SUBLANES - 1:SUBLANES - 1 + tm, :]
    g2 = gpad_ref[SUBLANES - 2:SUBLANES - 2 + tm, :]
    conv = g2 * cw_ref[0:1, :] + g1 * cw_ref[1:2, :] + gate * cw_ref[2:3, :] + cb_ref[...]
    act = (_silu(conv) * up).astype(BF16)
    acc_ref[...] += jnp.dot(act, wd_ref[...], preferred_element_type=F32)

    @pl.when(f == pl.num_programs(1) - 1)
    def _():
        o_ref[...] = h_ref[...] + _rms_scale(acc_ref[...], gpost_ref[...])


def conv_ffn(h, gpre, w_up, conv_w, conv_b, w_down, gpost, *, seq, tm, tf):
    m, d = h.shape
    dff = w_down.shape[0]
    nf = dff // tf
    kern = functools.partial(_ffn_kernel, tm=tm, tiles_per_seq=seq // tm)
    return pl.pallas_call(
        kern,
        grid=(m // tm, nf),
        in_specs=[
            pl.BlockSpec((tm, d), lambda i, f: (i, 0)),
            pl.BlockSpec((1, d), lambda i, f: (0, 0)),
            pl.BlockSpec((d, tf), lambda i, f: (0, f)),
            pl.BlockSpec((d, tf), lambda i, f: (0, f + nf)),
            pl.BlockSpec((FFN_CONV, tf), lambda i, f: (0, f)),
            pl.BlockSpec((1, tf), lambda i, f: (0, f)),
            pl.BlockSpec((tf, d), lambda i, f: (f, 0)),
            pl.BlockSpec((1, d), lambda i, f: (0, 0)),
        ],
        out_specs=pl.BlockSpec((tm, d), lambda i, f: (i, 0)),
        out_shape=jax.ShapeDtypeStruct((m, d), F32),
        scratch_shapes=[
            pltpu.VMEM((tm, d), BF16),
            pltpu.VMEM((tm, d), F32),
            pltpu.VMEM((tm + SUBLANES, tf), F32),
            pltpu.VMEM((nf, SUBLANES, tf), F32),
        ],
        compiler_params=_params("arbitrary", "arbitrary"),
        name="conv_ffn",
    )(h, gpre.reshape(1, d), w_up, w_up, conv_w, conv_b.reshape(1, dff), w_down, gpost.reshape(1, d))


def _gates_kernel(ba_ref, prm_ref, g_ref, gt_ref):
    ba = ba_ref[...]
    lane = lax.broadcasted_iota(jnp.int32, ba.shape, 1)
    is_beta = lane < GDN_HEADS
    is_g = jnp.logical_and(lane >= GDN_HEADS, lane < 2 * GDN_HEADS)
    g = -jnp.exp(prm_ref[0:1, :]) * jax.nn.softplus(ba + prm_ref[1:2, :])
    g = jnp.where(is_g, g, 0.0)
    c = ba.shape[0]
    row = lax.broadcasted_iota(jnp.int32, (c, c), 0)
    col = lax.broadcasted_iota(jnp.int32, (c, c), 1)
    tril = (row >= col).astype(F32)
    gcum = jnp.dot(tril, g, preferred_element_type=F32, precision=lax.Precision.HIGHEST)
    out = jnp.where(is_beta, jax.nn.sigmoid(ba), gcum)
    g_ref[...] = out
    gt_ref[0] = out.T


def gdn_gates(ba, prm):
    t = ba.shape[0]
    c = DELTA_CHUNK
    return pl.pallas_call(
        _gates_kernel,
        grid=(t // c,),
        in_specs=[
            pl.BlockSpec((c, LANES), lambda i: (i, 0)),
            pl.BlockSpec((SUBLANES, LANES), lambda i: (0, 0)),
        ],
        out_specs=[
            pl.BlockSpec((c, LANES), lambda i: (i, 0)),
            pl.BlockSpec((1, LANES, c), lambda i: (i, 0, 0)),
        ],
        out_shape=[
            jax.ShapeDtypeStruct((t, LANES), F32),
            jax.ShapeDtypeStruct((t // c, LANES, c), F32),
        ],
        compiler_params=_params("parallel"),
        name="gdn_gates",
    )(ba, prm)


def _unit_lower_inverse(lows, masks):
    eye, base_mask, merge_masks = masks
    a_s = [jnp.where(base_mask, -low, 0.0) for low in lows]
    xs = [eye + a for a in a_s]
    span = 2
    while span < INV_BASE:
        a_s = [_bdot(a, a) for a in a_s]
        xs = [x + _bdot(x, a) for x, a in zip(xs, a_s)]
        span *= 2
    for mm in merge_masks:
        ys = [_bdot(x, jnp.where(mm, low, 0.0)) for x, low in zip(xs, lows)]
        xs = [x - _bdot(y, x) for x, y in zip(xs, ys)]
    return xs


def _inverse_masks(c):
    row = lax.broadcasted_iota(jnp.int32, (c, c), 0)
    col = lax.broadcasted_iota(jnp.int32, (c, c), 1)
    eye = (row == col).astype(F32)
    same = lambda b: (row // b) == (col // b)
    base_mask = same(INV_BASE)
    merge_masks = []
    b = INV_BASE
    while b < c:
        merge_masks.append(jnp.logical_and(same(2 * b), jnp.logical_not(same(b))))
        b *= 2
    return eye, base_mask, merge_masks


def _gdn_kernel(q_ref, k_ref, v_ref, z_ref, g_ref, gt_ref, cw_ref, nw_ref, o_ref,
                xpad_ref, state_ref, u_ref, wq_ref, kd_ref, ai_ref, eg_ref, *, hb, ts, chunks_a):
    c = DELTA_CHUNK
    nchunk = ts // c
    hg = pl.program_id(1)
    tb = pl.program_id(2)
    halo = SUBLANES

    @pl.when(tb == 0)
    def _():
        xpad_ref[:, 0:halo, :] = jnp.zeros((3, halo, hb * LANES), F32)
        state_ref[...] = jnp.zeros_like(state_ref)

    @pl.when(tb > 0)
    def _():
        xpad_ref[:, 0:halo, :] = xpad_ref[:, ts:ts + halo, :]

    xpad_ref[0, halo:, :] = q_ref[...].astype(F32)
    xpad_ref[1, halo:, :] = k_ref[...].astype(F32)
    xpad_ref[2, halo:, :] = v_ref[...].astype(F32)

    row = lax.broadcasted_iota(jnp.int32, (c, c), 0)
    col = lax.broadcasted_iota(jnp.int32, (c, c), 1)
    causal = row >= col
    strict = row > col
    inv_masks = _inverse_masks(c)
    lane = lax.broadcasted_iota(jnp.int32, (c, LANES), 1)
    q_scale = GDN_DK ** -0.5

    def conv_silu(t, r0, i):
        lanes = slice(i * LANES, (i + 1) * LANES)
        win = xpad_ref[t, pl.ds(r0, c + halo), lanes]
        acc = None
        for j in range(GDN_CONV):
            off = halo - (GDN_CONV - 1) + j
            term = win[off:off + c, :] * cw_ref[j, t:t + 1, lanes]
            acc = term if acc is None else acc + term
        return _silu(acc)

    def l2norm(x):
        return x * lax.rsqrt(jnp.sum(x * x, axis=-1, keepdims=True) + EPS)

    def phase_a(it, carry):
        items = []
        for uu in range(chunks_a):
            ci = it * chunks_a + uu
            r0 = pl.multiple_of(ci * c, c)
            gates = g_ref[pl.ds(r0, c), :]
            for i in range(hb):
                head = hg * hb + i
                beta = jnp.sum(jnp.where(lane == head, gates, 0.0), axis=-1, keepdims=True)
                gc = jnp.sum(jnp.where(lane == head + GDN_HEADS, gates, 0.0), axis=-1, keepdims=True)
                gc_row = gt_ref[ci, pl.ds(head + GDN_HEADS, 1), :]
                items.append(dict(ci=ci, r0=r0, i=i, lanes=slice(i * LANES, (i + 1) * LANES),
                                  beta=beta, gc=gc, gc_row=gc_row, g_last=gc_row[:, c - 1:c]))
        for d in items:
            d["qn"] = l2norm(conv_silu(0, d["r0"], d["i"])) * q_scale
            d["kn"] = l2norm(conv_silu(1, d["r0"], d["i"]))
            d["kb"] = d["kn"] * d["beta"]
        for d in items:
            d["qk"] = lax.dot_general(jnp.concatenate([d["qn"], d["kb"]], axis=0).astype(BF16),
                                      d["kn"].astype(BF16), NT_DIMS, preferred_element_type=F32)
        lows = []
        for d in items:
            decay = jnp.exp(jnp.where(causal, d["gc"] - d["gc_row"], -jnp.inf))
            ai_ref[pl.ds(d["r0"], c), d["lanes"]] = jnp.where(causal, d["qk"][:c] * decay, 0.0).astype(BF16)
            lows.append(jnp.where(strict, d["qk"][c:] * decay, 0.0))
        tinvs = _unit_lower_inverse(lows, inv_masks)
        rhss = []
        for d in items:
            d["egc"] = jnp.exp(d["gc"])
            vv = conv_silu(2, d["r0"], d["i"])
            rhss.append(jnp.concatenate([vv * d["beta"], d["kb"] * d["egc"]], axis=1))
        sols = [_bdot(tinv, rhs) for tinv, rhs in zip(tinvs, rhss)]
        for d, sol in zip(items, sols):
            r0, lanes = d["r0"], d["lanes"]
            u_ref[pl.ds(r0, c), lanes] = sol[:, :LANES]
            wq_ref[0, pl.ds(r0, c), lanes] = sol[:, LANES:].astype(BF16)
            wq_ref[1, pl.ds(r0, c), lanes] = (d["qn"] * d["egc"]).astype(BF16)
            kd_ref[pl.ds(r0, c), lanes] = (d["kn"] * jnp.exp(d["g_last"] - d["gc"])).astype(BF16)
            eg_ref[d["ci"], d["i"]] = jnp.broadcast_to(jnp.exp(d["g_last"]), (SUBLANES, LANES))
        return carry

    lax.fori_loop(0, nchunk // chunks_a, phase_a, 0)

    def phase_b(ci, carry):
        r0 = pl.multiple_of(ci * c, c)
        heads = range(hb)
        lanes = [slice(i * LANES, (i + 1) * LANES) for i in heads]
        states = [state_ref[i] for i in heads]
        wss = [jnp.dot(jnp.concatenate([wq_ref[0, pl.ds(r0, c), lanes[i]], wq_ref[1, pl.ds(r0, c), lanes[i]]],
                                       axis=0),
                       states[i].astype(BF16), preferred_element_type=F32) for i in heads]
        v_news = [(u_ref[pl.ds(r0, c), lanes[i]] - wss[i][:c]).astype(BF16) for i in heads]
        upds = [lax.dot_general(kd_ref[pl.ds(r0, c), lanes[i]], v_news[i], TN_DIMS, preferred_element_type=F32)
                for i in heads]
        for i in heads:
            state_ref[i] = states[i] * eg_ref[ci, i][0:1, :] + upds[i]
        outs = [wss[i][c:] + jnp.dot(ai_ref[pl.ds(r0, c), lanes[i]], v_news[i], preferred_element_type=F32)
                for i in heads]
        for i in heads:
            z = z_ref[pl.ds(r0, c), lanes[i]].astype(F32)
            o_ref[pl.ds(r0, c), lanes[i]] = (_rms_scale(outs[i], nw_ref[...]) * _silu(z)).astype(o_ref.dtype)
        return carry

    lax.fori_loop(0, nchunk, phase_b, 0)


def gdn_core(proj, gates, gates_t, conv_w, norm_w, *, batch, seq, hb, ts, chunks_a):
    t = proj.shape[0]
    w = hb * LANES
    ngroups = GDN_HEADS // hb
    nt = seq // ts
    nchunk = ts // DELTA_CHUNK
    kern = functools.partial(_gdn_kernel, hb=hb, ts=ts, chunks_a=chunks_a)

    def col_spec(base):
        return pl.BlockSpec((ts, w), lambda b, h, s: (b * nt + s, base * ngroups + h))

    return pl.pallas_call(
        kern,
        grid=(batch, ngroups, nt),
        in_specs=[
            col_spec(0), col_spec(1), col_spec(2), col_spec(3),
            pl.BlockSpec((ts, LANES), lambda b, h, s: (b * nt + s, 0)),
            pl.BlockSpec((nchunk, LANES, DELTA_CHUNK), lambda b, h, s: (b * nt + s, 0, 0)),
            pl.BlockSpec((GDN_CONV, 3, w), lambda b, h, s: (0, 0, h)),
            pl.BlockSpec((1, LANES), lambda b, h, s: (0, 0)),
        ],
        out_specs=pl.BlockSpec((ts, w), lambda b, h, s: (b * nt + s, h)),
        out_shape=jax.ShapeDtypeStruct((t, GDN_V_DIM), BF16),
        scratch_shapes=[
            pltpu.VMEM((3, ts + SUBLANES, w), F32),
            pltpu.VMEM((hb, GDN_DK, GDN_DV), F32),
            pltpu.VMEM((ts, w), F32),
            pltpu.VMEM((2, ts, w), BF16),
            pltpu.VMEM((ts, w), BF16),
            pltpu.VMEM((ts, w), BF16),
            pltpu.VMEM((nchunk, hb, SUBLANES, LANES), F32),
        ],
        compiler_params=_params("parallel", "parallel", "arbitrary"),
        name="gdn_core",
    )(proj, proj, proj, proj, gates, gates_t, conv_w, norm_w.reshape(1, LANES))


def _attn_kernel(qi_tab, ki_tab, q_ref, k_ref, v_ref, lam_ref, sw_ref, o_ref,
                 m_ref, l_ref, acc_ref, *, tq, tk, lambda_init):
    p = pl.program_id(2)
    qi = qi_tab[p]
    ki = ki_tab[p]
    dh = DIFF_DH

    @pl.when(ki == 0)
    def _():
        m_ref[...] = jnp.full_like(m_ref, -jnp.inf)
        l_ref[...] = jnp.zeros_like(l_ref)
        acc_ref[...] = jnp.zeros_like(acc_ref)

    def update(masked):
        q = q_ref[...]
        k = k_ref[...]
        v = v_ref[...]
        if masked:
            q_chunk = (qi * tq + lax.broadcasted_iota(jnp.int32, (tq, tk), 0)) // CHUNK
            k_chunk = (ki * tk + lax.broadcasted_iota(jnp.int32, (tq, tk), 1)) // CHUNK
            visible = k_chunk <= q_chunk
        ss = [lax.dot_general(q[:, m * dh:(m + 1) * dh], k[:, m * dh:(m + 1) * dh], NT_DIMS,
                              preferred_element_type=F32) for m in range(2)]
        for m in range(2):
            s = ss[m]
            if masked:
                s = jnp.where(visible, s, -jnp.inf)
            m_old = m_ref[m]
            m_new = jnp.maximum(m_old, jnp.max(s, axis=-1, keepdims=True))
            alpha = jnp.exp2(m_old - m_new)
            pr = jnp.exp2(s - jnp.concatenate([m_new] * (tk // LANES), axis=1))
            psum = pr[:, 0:LANES]
            for t in range(1, tk // LANES):
                psum = psum + pr[:, t * LANES:(t + 1) * LANES]
            l_ref[m] = alpha * l_ref[m] + psum
            acc_ref[m] = (jnp.concatenate([alpha] * (2 * dh // LANES), axis=1) * acc_ref[m]
                          + jnp.dot(pr.astype(BF16), v, preferred_element_type=F32))
            m_ref[m] = m_new

    fully_visible = (ki + 1) * tk <= qi * tq + CHUNK
    pl.when(fully_visible)(functools.partial(update, False))
    pl.when(jnp.logical_not(fully_visible))(functools.partial(update, True))

    @pl.when((ki + 1) * tk >= (qi + 1) * tq)
    def _():
        lam = (jnp.exp(jnp.sum(lam_ref[0:1, :] * lam_ref[1:2, :], axis=-1, keepdims=True))
               - jnp.exp(jnp.sum(lam_ref[2:3, :] * lam_ref[3:4, :], axis=-1, keepdims=True))
               + lambda_init)
        l0 = jnp.sum(l_ref[0], axis=-1, keepdims=True)
        l1 = jnp.sum(l_ref[1], axis=-1, keepdims=True)
        a = acc_ref[0] / l0 - lam * (acc_ref[1] / l1)
        o_ref[...] = (_rms_scale(a, sw_ref[...]) * (1.0 - lambda_init)).astype(o_ref.dtype)


def diff_attention(q, kv, lam_rows, subln_w, *, batch, seq, tq, tk, lambda_init):
    t = q.shape[0]
    hw = 2 * DIFF_DH
    nq = seq // tq
    nk = seq // tk
    pairs = [(a, b) for a in range(nq) for b in range(nk) if b * tk < (a + 1) * tq]
    qi_tab = jnp.asarray([a for a, _ in pairs], jnp.int32)
    ki_tab = jnp.asarray([b for _, b in pairs], jnp.int32)
    kern = functools.partial(_attn_kernel, tq=tq, tk=tk, lambda_init=lambda_init)
    grid_spec = pltpu.PrefetchScalarGridSpec(
        num_scalar_prefetch=2,
        grid=(batch, DIFF_HEADS, len(pairs)),
        in_specs=[
            pl.BlockSpec((tq, hw), lambda b, h, p, qt, kt: (b * nq + qt[p], h)),
            pl.BlockSpec((tk, hw), lambda b, h, p, qt, kt: (b * nk + kt[p], h)),
            pl.BlockSpec((tk, hw), lambda b, h, p, qt, kt: (b * nk + kt[p], DIFF_HEADS + h)),
            pl.BlockSpec((SUBLANES, DIFF_DH), lambda b, h, p, qt, kt: (0, 0)),
            pl.BlockSpec((1, hw), lambda b, h, p, qt, kt: (0, 0)),
        ],
        out_specs=pl.BlockSpec((tq, hw), lambda b, h, p, qt, kt: (b * nq + qt[p], h)),
        scratch_shapes=[
            pltpu.VMEM((2, tq, LANES), F32),
            pltpu.VMEM((2, tq, LANES), F32),
            pltpu.VMEM((2, tq, hw), F32),
        ],
    )
    return pl.pallas_call(
        kern,
        grid_spec=grid_spec,
        out_shape=jax.ShapeDtypeStruct((t, DIFF_HEADS * hw), BF16),
        compiler_params=_params("parallel", "parallel", "arbitrary"),
        name="diff_attention",
    )(qi_tab, ki_tab, q, kv, kv, lam_rows, subln_w.reshape(1, hw))


def kernel(x, norm_mix_pre, norm_mix_post, norm_ffn_pre, norm_ffn_post, gdn_w_in, gdn_conv_w, gdn_a_log,
           gdn_dt_bias, gdn_norm_w, gdn_w_out, kv_norm_w, w_kv, diff_w_q, diff_lq1, diff_lk1, diff_lq2,
           diff_lk2, diff_subln_w, diff_w_o, ffn_w_up, ffn_conv_w, ffn_conv_b, ffn_w_down):
    batch, seq, d = x.shape
    t = batch * seq
    h = x.reshape(t, d)
    n_main = 2 * GDN_QK_DIM + 2 * GDN_V_DIM

    def ffn(h, layer):
        return conv_ffn(h, norm_ffn_pre[layer], ffn_w_up[layer].astype(BF16), ffn_conv_w[layer],
                        ffn_conv_b[layer], ffn_w_down[layer].astype(BF16), norm_ffn_post[layer],
                        seq=seq, tm=512, tf=512)

    for layer in range(DEPTH):
        if layer < N_A:
            w_in = gdn_w_in[layer]
            proj = norm_matmul(h, norm_mix_pre[layer], w_in[:, :n_main].astype(BF16), BF16, tm=1024, tn=512)
            w_ba = jnp.pad(w_in[:, n_main:], ((0, 0), (0, LANES - 2 * GDN_HEADS))).astype(BF16)
            ba = norm_matmul(h, norm_mix_pre[layer], w_ba, F32, tm=1024, tn=LANES)
            prm = jnp.zeros((SUBLANES, LANES), F32)
            prm = prm.at[0, GDN_HEADS:2 * GDN_HEADS].set(gdn_a_log[layer].astype(F32))
            prm = prm.at[1, GDN_HEADS:2 * GDN_HEADS].set(gdn_dt_bias[layer].astype(F32))
            gates, gates_t = gdn_gates(ba, prm)
            conv_w = gdn_conv_w[layer].reshape(GDN_CONV, 3, GDN_QK_DIM)
            o = gdn_core(proj, gates, gates_t, conv_w, gdn_norm_w[layer], batch=batch, seq=seq, hb=4, ts=1024,
                         chunks_a=2)
            h = matmul_norm_res(o, gdn_w_out[layer].astype(BF16), h, norm_mix_post[layer], tm=512)
        else:
            j = layer - N_A
            if layer == N_A:
                kv = norm_matmul(h, kv_norm_w, w_kv.astype(BF16), BF16, tm=1024, tn=512)
            lambda_init = 0.8 - 0.6 * math.exp(-0.3 * layer)
            q = norm_matmul(h, norm_mix_pre[layer], diff_w_q[j].astype(BF16), BF16, tm=1024, tn=512,
                            out_scale=DIFF_DH ** -0.5 * math.log2(math.e))
            lam_rows = jnp.zeros((SUBLANES, DIFF_DH), F32)
            lam_rows = lam_rows.at[0].set(diff_lq1[j]).at[1].set(diff_lk1[j])
            lam_rows = lam_rows.at[2].set(diff_lq2[j]).at[3].set(diff_lk2[j])
            att = diff_attention(q, kv, lam_rows, diff_subln_w[j], batch=batch, seq=seq, tq=512, tk=512,
                                 lambda_init=lambda_init)
            h = matmul_norm_res(att, diff_w_o[j].astype(BF16), h, norm_mix_post[layer], tm=512)
        h = ffn(h, layer)
    return h.reshape(batch, seq, d)
```

```python
import functools
import math

import jax
import jax.numpy as jnp
from jax import lax
from jax.experimental import pallas as pl
from jax.experimental.pallas import tpu as pltpu

D_MODEL = 2048
DEPTH = 2
N_A = DEPTH // 2
CHUNK = 64
GDN_HEADS = 16
GDN_DK = 128
GDN_DV = 128
GDN_CONV = 4
GDN_QK_DIM = GDN_HEADS * GDN_DK
GDN_V_DIM = GDN_HEADS * GDN_DV
DIFF_HEADS = 8
DIFF_DH = 128
DIFF_QK_DIM = DIFF_HEADS * 2 * DIFF_DH
D_FF = 5632
FFN_CONV = 3
EPS = 1e-6

LANES = 128
SUBLANES = 8
MXU_COLS = 256
VMEM_LIMIT_BYTES = 56 * 1024 * 1024

DELTA_CHUNK = 128
INV_BASE = 16

F32 = jnp.float32
BF16 = jnp.bfloat16
NT_DIMS = (((1,), (1,)), ((), ()))
TN_DIMS = (((0,), (0,)), ((), ()))


def _bdot(a, b):
    return jnp.dot(a.astype(BF16), b.astype(BF16), preferred_element_type=F32)


def _rms_scale(x, gain):
    ms = jnp.mean(x * x, axis=-1, keepdims=True)
    return x * lax.rsqrt(ms + EPS) * gain


def _silu(x):
    return x * jax.nn.sigmoid(x)


def _params(*sem):
    return pltpu.CompilerParams(dimension_semantics=sem, vmem_limit_bytes=VMEM_LIMIT_BYTES)


def _norm_matmul_kernel(*refs, split, out_scale, has_extra):
    if has_extra:
        x_ref, g_ref, w_ref, w2_ref, o_ref, o2_ref, xn_ref = refs
    else:
        x_ref, g_ref, w_ref, o_ref, xn_ref = refs
    j = pl.program_id(1)
    nseg = g_ref.shape[0]

    @pl.when(j == 0)
    def _():
        x = x_ref[...]
        xh = x * lax.rsqrt(jnp.mean(x * x, axis=-1, keepdims=True) + EPS)
        for s in range(nseg):
            xn_ref[s] = (xh * g_ref[s:s + 1, :]).astype(BF16)
        if has_extra:
            o2_ref[...] = jnp.dot(xn_ref[0], w2_ref[...], preferred_element_type=F32)

    if nseg == 1:
        y = jnp.dot(xn_ref[0], w_ref[...], preferred_element_type=F32)
        if out_scale is not None:
            y = y * out_scale
    else:
        second = j >= split
        y = jnp.dot(xn_ref[second.astype(jnp.int32)], w_ref[...], preferred_element_type=F32)
        if out_scale is not None:
            y = y * jnp.where(second, out_scale, 1.0)
    o_ref[...] = y.astype(o_ref.dtype)


def norm_matmul(x, gains, w, out_dtype, *, tm, tn, n_out=None, split=None, out_scale=None, w_extra=None):
    m, k = x.shape
    n = w.shape[1] if n_out is None else n_out
    nseg = gains.shape[0]
    has_extra = w_extra is not None
    in_specs = [
        pl.BlockSpec((tm, k), lambda i, j: (i, 0)),
        pl.BlockSpec((nseg, k), lambda i, j: (0, 0)),
        pl.BlockSpec((k, tn), lambda i, j: (0, j)),
    ]
    out_specs = pl.BlockSpec((tm, tn), lambda i, j: (i, j))
    out_shape = jax.ShapeDtypeStruct((m, n), out_dtype)
    args = [x, gains, w]
    if has_extra:
        in_specs.append(pl.BlockSpec((k, LANES), lambda i, j: (0, 0)))
        out_specs = [out_specs, pl.BlockSpec((tm, LANES), lambda i, j: (i, 0))]
        out_shape = [out_shape, jax.ShapeDtypeStruct((m, LANES), F32)]
        args.append(w_extra)
    return pl.pallas_call(
        functools.partial(_norm_matmul_kernel, split=split, out_scale=out_scale, has_extra=has_extra),
        grid=(m // tm, n // tn),
        in_specs=in_specs,
        out_specs=out_specs,
        out_shape=out_shape,
        scratch_shapes=[pltpu.VMEM((nseg, tm, k), BF16)],
        compiler_params=_params("parallel", "arbitrary"),
        name="norm_matmul",
    )(*args)


def _matmul_norm_res_kernel(a_ref, w_ref, h_ref, g_ref, o_ref):
    m = jnp.dot(a_ref[...], w_ref[...], preferred_element_type=F32)
    o_ref[...] = h_ref[...] + _rms_scale(m, g_ref[...])


def matmul_norm_res(a, w, h, gain, *, tm):
    m, k = a.shape
    n = w.shape[1]
    return pl.pallas_call(
        _matmul_norm_res_kernel,
        grid=(m // tm,),
        in_specs=[
            pl.BlockSpec((tm, k), lambda i: (i, 0)),
            pl.BlockSpec((k, n), lambda i: (0, 0)),
            pl.BlockSpec((tm, n), lambda i: (i, 0)),
            pl.BlockSpec((1, n), lambda i: (0, 0)),
        ],
        out_specs=pl.BlockSpec((tm, n), lambda i: (i, 0)),
        out_shape=jax.ShapeDtypeStruct((m, n), F32),
        compiler_params=_params("parallel"),
        name="matmul_norm_res",
    )(a, w, h, gain.reshape(1, n))


def _ffn_kernel(h_ref, gpre_ref, wg_ref, wu_ref, cw_ref, cb_ref, wd_ref, gpost_ref, o_ref,
                xn_ref, acc_ref, carry_ref, *, tm, tiles_per_seq):
    i = pl.program_id(0)
    f = pl.program_id(1)

    @pl.when(f == 0)
    def _():
        xn_ref[...] = _rms_scale(h_ref[...], gpre_ref[...]).astype(BF16)
        acc_ref[...] = jnp.zeros_like(acc_ref)

    xn = xn_ref[...]
    gate = jnp.dot(xn, wg_ref[...], preferred_element_type=F32)
    up = jnp.dot(xn, wu_ref[...], preferred_element_type=F32)

    seq_start = (i % tiles_per_seq) == 0
    prev = jnp.where(seq_start, 0.0, carry_ref[f])
    carry_ref[f] = gate[tm - SUBLANES:, :]
    head = jnp.concatenate([prev, gate[0:SUBLANES, :]], axis=0)

    def delayed(shift):
        body = pltpu.roll(gate, shift, axis=0)
        first = pltpu.roll(head, shift, axis=0)[SUBLANES:, :]
        return jnp.concatenate([first, body[SUBLANES:, :]], axis=0)

    conv = delayed(2) * cw_ref[0:1, :] + delayed(1) * cw_ref[1:2, :] + gate * cw_ref[2:3, :] + cb_ref[...]
    act = (_silu(conv) * up).astype(BF16)
    acc_ref[...] += jnp.dot(act, wd_ref[...], preferred_element_type=F32)

    @pl.when(f == pl.num_programs(1) - 1)
    def _():
        o_ref[...] = h_ref[...] + _rms_scale(acc_ref[...], gpost_ref[...])


def conv_ffn(h, gpre, w_up, conv_w, conv_b, w_down, gpost, *, seq, tm, tf):
    m, d = h.shape
    dff = w_down.shape[0]
    nf = dff // tf
    kern = functools.partial(_ffn_kernel, tm=tm, tiles_per_seq=seq // tm)
    return pl.pallas_call(
        kern,
        grid=(m // tm, nf),
        in_specs=[
            pl.BlockSpec((tm, d), lambda i, f: (i, 0)),
            pl.BlockSpec((1, d), lambda i, f: (0, 0)),
            pl.BlockSpec((d, tf), lambda i, f: (0, f)),
            pl.BlockSpec((d, tf), lambda i, f: (0, f + nf)),
            pl.BlockSpec((FFN_CONV, tf), lambda i, f: (0, f)),
            pl.BlockSpec((1, tf), lambda i, f: (0, f)),
            pl.BlockSpec((tf, d), lambda i, f: (f, 0)),
            pl.BlockSpec((1, d), lambda i, f: (0, 0)),
        ],
        out_specs=pl.BlockSpec((tm, d), lambda i, f: (i, 0)),
        out_shape=jax.ShapeDtypeStruct((m, d), F32),
        scratch_shapes=[
            pltpu.VMEM((tm, d), BF16),
            pltpu.VMEM((tm, d), F32),
            pltpu.VMEM((nf, SUBLANES, tf), F32),
        ],
        compiler_params=_params("arbitrary", "arbitrary"),
        name="conv_ffn",
    )(h, gpre.reshape(1, d), w_up, w_up, conv_w, conv_b.reshape(1, dff), w_down, gpost.reshape(1, d))


def _gates_kernel(ba_ref, prm_ref, g_ref, gt_ref):
    ba = ba_ref[...]
    lane = lax.broadcasted_iota(jnp.int32, ba.shape, 1)
    is_beta = lane < GDN_HEADS
    is_g = jnp.logical_and(lane >= GDN_HEADS, lane < 2 * GDN_HEADS)
    g = -jnp.exp(prm_ref[0:1, :]) * jax.nn.softplus(ba + prm_ref[1:2, :])
    g = jnp.where(is_g, g, 0.0)
    c = ba.shape[0]
    row = lax.broadcasted_iota(jnp.int32, (c, c), 0)
    col = lax.broadcasted_iota(jnp.int32, (c, c), 1)
    tril = (row >= col).astype(F32)
    gcum = jnp.dot(tril, g, preferred_element_type=F32, precision=lax.Precision.HIGHEST)
    out = jnp.where(is_beta, jax.nn.sigmoid(ba), gcum)
    g_ref[...] = out
    gt_ref[0] = out.T


def gdn_gates(ba, prm):
    t = ba.shape[0]
    c = DELTA_CHUNK
    return pl.pallas_call(
        _gates_kernel,
        grid=(t // c,),
        in_specs=[
            pl.BlockSpec((c, LANES), lambda i: (i, 0)),
            pl.BlockSpec((SUBLANES, LANES), lambda i: (0, 0)),
        ],
        out_specs=[
            pl.BlockSpec((c, LANES), lambda i: (i, 0)),
            pl.BlockSpec((1, LANES, c), lambda i: (i, 0, 0)),
        ],
        out_shape=[
            jax.ShapeDtypeStruct((t, LANES), F32),
            jax.ShapeDtypeStruct((t // c, LANES, c), F32),
        ],
        compiler_params=_params("parallel"),
        name="gdn_gates",
    )(ba, prm)


def _unit_lower_inverse(lows, masks):
    eye, base_mask, merge_masks = masks
    a_s = [jnp.where(base_mask, -low, 0.0) for low in lows]
    xs = [eye + a for a in a_s]
    span = 2
    while span < INV_BASE:
        a_s = [_bdot(a, a) for a in a_s]
        xs = [x + _bdot(x, a) for x, a in zip(xs, a_s)]
        span *= 2
    for mm in merge_masks:
        ys = [_bdot(x, jnp.where(mm, low, 0.0)) for x, low in zip(xs, lows)]
        xs = [x - _bdot(y, x) for x, y in zip(xs, ys)]
    return xs


def _inverse_masks(c):
    row = lax.broadcasted_iota(jnp.int32, (c, c), 0)
    col = lax.broadcasted_iota(jnp.int32, (c, c), 1)
    eye = (row == col).astype(F32)
    same = lambda b: (row // b) == (col // b)
    base_mask = same(INV_BASE)
    merge_masks = []
    b = INV_BASE
    while b < c:
        merge_masks.append(jnp.logical_and(same(2 * b), jnp.logical_not(same(b))))
        b *= 2
    return eye, base_mask, merge_masks


def _gdn_kernel(q_ref, k_ref, v_ref, z_ref, g_ref, gt_ref, cw_ref, nw_ref, o_ref,
                xpad_ref, state_ref, u_ref, wq_ref, kd_ref, ai_ref, eg_ref, *, hb, ts, chunks_a):
    c = DELTA_CHUNK
    nchunk = ts // c
    hg = pl.program_id(1)
    tb = pl.program_id(2)
    halo = SUBLANES

    @pl.when(tb == 0)
    def _():
        xpad_ref[:, 0:halo, :] = jnp.zeros((3, halo, hb * LANES), F32)
        state_ref[...] = jnp.zeros_like(state_ref)

    @pl.when(tb > 0)
    def _():
        xpad_ref[:, 0:halo, :] = xpad_ref[:, ts:ts + halo, :]

    xpad_ref[0, halo:, :] = q_ref[...].astype(F32)
    xpad_ref[1, halo:, :] = k_ref[...].astype(F32)
    xpad_ref[2, halo:, :] = v_ref[...].astype(F32)

    row = lax.broadcasted_iota(jnp.int32, (c, c), 0)
    col = lax.broadcasted_iota(jnp.int32, (c, c), 1)
    causal = row >= col
    strict = row > col
    inv_masks = _inverse_masks(c)
    lane = lax.broadcasted_iota(jnp.int32, (c, LANES), 1)
    q_scale = GDN_DK ** -0.5

    def conv_silu(t, r0, i):
        lanes = slice(i * LANES, (i + 1) * LANES)
        win = xpad_ref[t, pl.ds(r0, c + halo), lanes]
        x = win[halo:, :]
        head = win[0:2 * halo, :]
        acc = None
        for j in range(GDN_CONV):
            delay = GDN_CONV - 1 - j
            if delay == 0:
                xd = x
            else:
                xd = jnp.concatenate([pltpu.roll(head, delay, axis=0)[halo:, :],
                                      pltpu.roll(x, delay, axis=0)[halo:, :]], axis=0)
            term = xd * cw_ref[j, t:t + 1, lanes]
            acc = term if acc is None else acc + term
        return _silu(acc)

    def l2norm(x):
        return x * lax.rsqrt(jnp.sum(x * x, axis=-1, keepdims=True) + EPS)

    def phase_a(it, carry):
        items = []
        for uu in range(chunks_a):
            ci = it * chunks_a + uu
            r0 = pl.multiple_of(ci * c, c)
            gates = g_ref[pl.ds(r0, c), :]
            for i in range(hb):
                head = hg * hb + i
                beta = jnp.sum(jnp.where(lane == head, gates, 0.0), axis=-1, keepdims=True)
                gc = jnp.sum(jnp.where(lane == head + GDN_HEADS, gates, 0.0), axis=-1, keepdims=True)
                gc_row = gt_ref[ci, pl.ds(head + GDN_HEADS, 1), :]
                items.append(dict(ci=ci, r0=r0, i=i, lanes=slice(i * LANES, (i + 1) * LANES),
                                  beta=beta, gc=gc, gc_row=gc_row, g_last=gc_row[:, c - 1:c]))
        for d in items:
            d["qn"] = l2norm(conv_silu(0, d["r0"], d["i"])) * q_scale
            d["kn"] = l2norm(conv_silu(1, d["r0"], d["i"]))
            d["kb"] = d["kn"] * d["beta"]
        for d in items:
            d["qk"] = lax.dot_general(jnp.concatenate([d["qn"], d["kb"]], axis=0).astype(BF16),
                                      d["kn"].astype(BF16), NT_DIMS, preferred_element_type=F32)
        lows = []
        for d in items:
            decay = jnp.exp(jnp.where(causal, d["gc"] - d["gc_row"], -jnp.inf))
            ai_ref[pl.ds(d["r0"], c), d["lanes"]] = jnp.where(causal, d["qk"][:c] * decay, 0.0).astype(BF16)
            lows.append(jnp.where(strict, d["qk"][c:] * decay, 0.0))
        tinvs = _unit_lower_inverse(lows, inv_masks)
        rhss = []
        for d in items:
            d["egc"] = jnp.exp(d["gc"])
            vv = conv_silu(2, d["r0"], d["i"])
            rhss.append(jnp.concatenate([vv * d["beta"], d["kb"] * d["egc"]], axis=1))
        sols = [_bdot(tinv, rhs) for tinv, rhs in zip(tinvs, rhss)]
        for d, sol in zip(items, sols):
            r0, lanes = d["r0"], d["lanes"]
            u_ref[pl.ds(r0, c), lanes] = sol[:, :LANES]
            wq_ref[0, pl.ds(r0, c), lanes] = sol[:, LANES:].astype(BF16)
            wq_ref[1, pl.ds(r0, c), lanes] = (d["qn"] * d["egc"]).astype(BF16)
            kd_ref[pl.ds(r0, c), lanes] = (d["kn"] * jnp.exp(d["g_last"] - d["gc"])).astype(BF16)
            eg_ref[d["ci"], d["i"]] = jnp.broadcast_to(jnp.exp(d["g_last"]), (SUBLANES, LANES))
        return carry

    lax.fori_loop(0, nchunk // chunks_a, phase_a, 0)

    def phase_b(ci, carry):
        r0 = pl.multiple_of(ci * c, c)
        heads = range(hb)
        lanes = [slice(i * LANES, (i + 1) * LANES) for i in heads]
        states = [state_ref[i] for i in heads]
        wss = [jnp.dot(jnp.concatenate([wq_ref[0, pl.ds(r0, c), lanes[i]], wq_ref[1, pl.ds(r0, c), lanes[i]]],
                                       axis=0),
                       states[i].astype(BF16), preferred_element_type=F32) for i in heads]
        v_news = [(u_ref[pl.ds(r0, c), lanes[i]] - wss[i][:c]).astype(BF16) for i in heads]
        upds = [lax.dot_general(kd_ref[pl.ds(r0, c), lanes[i]], v_news[i], TN_DIMS, preferred_element_type=F32)
                for i in heads]
        for i in heads:
            state_ref[i] = states[i] * eg_ref[ci, i][0:1, :] + upds[i]
        outs = [wss[i][c:] + jnp.dot(ai_ref[pl.ds(r0, c), lanes[i]], v_news[i], preferred_element_type=F32)
                for i in heads]
        for i in heads:
            z = z_ref[pl.ds(r0, c), lanes[i]].astype(F32)
            o_ref[pl.ds(r0, c), lanes[i]] = (_rms_scale(outs[i], nw_ref[...]) * _silu(z)).astype(o_ref.dtype)
        return carry

    lax.fori_loop(0, nchunk, phase_b, 0)


def gdn_core(proj, gates, gates_t, conv_w, norm_w, *, batch, seq, hb, ts, chunks_a):
    t = proj.shape[0]
    w = hb * LANES
    ngroups = GDN_HEADS // hb
    nt = seq // ts
    nchunk = ts // DELTA_CHUNK
    kern = functools.partial(_gdn_kernel, hb=hb, ts=ts, chunks_a=chunks_a)

    def col_spec(base):
        return pl.BlockSpec((ts, w), lambda b, h, s: (b * nt + s, base * ngroups + h))

    return pl.pallas_call(
        kern,
        grid=(batch, ngroups, nt),
        in_specs=[
            col_spec(0), col_spec(1), col_spec(2), col_spec(3),
            pl.BlockSpec((ts, LANES), lambda b, h, s: (b * nt + s, 0)),
            pl.BlockSpec((nchunk, LANES, DELTA_CHUNK), lambda b, h, s: (b * nt + s, 0, 0)),
            pl.BlockSpec((GDN_CONV, 3, w), lambda b, h, s: (0, 0, h)),
            pl.BlockSpec((1, LANES), lambda b, h, s: (0, 0)),
        ],
        out_specs=pl.BlockSpec((ts, w), lambda b, h, s: (b * nt + s, h)),
        out_shape=jax.ShapeDtypeStruct((t, GDN_V_DIM), BF16),
        scratch_shapes=[
            pltpu.VMEM((3, ts + SUBLANES, w), F32),
            pltpu.VMEM((hb, GDN_DK, GDN_DV), F32),
            pltpu.VMEM((ts, w), F32),
            pltpu.VMEM((2, ts, w), BF16),
            pltpu.VMEM((ts, w), BF16),
            pltpu.VMEM((ts, w), BF16),
            pltpu.VMEM((nchunk, hb, SUBLANES, LANES), F32),
        ],
        compiler_params=_params("parallel", "parallel", "arbitrary"),
        name="gdn_core",
    )(proj, proj, proj, proj, gates, gates_t, conv_w, norm_w.reshape(1, LANES))


def _attn_kernel(qi_tab, ki_tab, q_ref, k_ref, v_ref, lam_ref, sw_ref, o_ref,
                 m_ref, l_ref, acc_ref, *, tq, tk, lambda_init):
    p = pl.program_id(2)
    qi = qi_tab[p]
    ki = ki_tab[p]
    dh = DIFF_DH

    @pl.when(ki == 0)
    def _():
        m_ref[...] = jnp.full_like(m_ref, -jnp.inf)
        l_ref[...] = jnp.zeros_like(l_ref)
        acc_ref[...] = jnp.zeros_like(acc_ref)

    parts = tq // tk
    row = lax.broadcasted_iota(jnp.int32, (tk, tk), 0)
    col = lax.broadcasted_iota(jnp.int32, (tk, tk), 1)
    diag_visible = (col // CHUNK) <= (row // CHUNK)

    def update(diag_part):
        k = k_ref[...]
        v = v_ref[...]
        first = 0 if diag_part is None else diag_part
        chains = [(r, m) for r in range(first, parts) for m in range(2)]
        def scores(r, m):
            return lax.dot_general(q_ref[r * tk:(r + 1) * tk, m * dh:(m + 1) * dh], k[:, m * dh:(m + 1) * dh],
                                   NT_DIMS, preferred_element_type=F32)

        s_next = scores(*chains[0])
        for idx, (r, m) in enumerate(chains):
            s = s_next
            if idx + 1 < len(chains):
                s_next = scores(*chains[idx + 1])
            rows = slice(r * tk, (r + 1) * tk)
            if r == diag_part:
                s = jnp.where(diag_visible, s, -jnp.inf)
            m_old = m_ref[m, rows, :]
            m_new = jnp.maximum(m_old, jnp.max(s, axis=-1, keepdims=True))
            alpha = jnp.exp2(m_old - m_new)
            pr = jnp.exp2(s - jnp.concatenate([m_new] * (tk // LANES), axis=1))
            psum = pr[:, 0:LANES]
            for t in range(1, tk // LANES):
                psum = psum + pr[:, t * LANES:(t + 1) * LANES]
            l_ref[m, rows, :] = alpha * l_ref[m, rows, :] + psum
            acc_ref[m, rows, :] = (jnp.concatenate([alpha] * (2 * dh // LANES), axis=1) * acc_ref[m, rows, :]
                                   + jnp.dot(pr.astype(BF16), v, preferred_element_type=F32))
            m_ref[m, rows, :] = m_new

    d = ki - qi * parts
    pl.when(d < 0)(functools.partial(update, None))
    for j in range(parts):
        pl.when(d == j)(functools.partial(update, j))

    @pl.when(d == parts - 1)
    def _():
        lam = (jnp.exp(jnp.sum(lam_ref[0:1, :] * lam_ref[1:2, :], axis=-1, keepdims=True))
               - jnp.exp(jnp.sum(lam_ref[2:3, :] * lam_ref[3:4, :], axis=-1, keepdims=True))
               + lambda_init)
        l0 = jnp.sum(l_ref[0], axis=-1, keepdims=True)
        l1 = jnp.sum(l_ref[1], axis=-1, keepdims=True)
        a = acc_ref[0] / l0 - lam * (acc_ref[1] / l1)
        o_ref[...] = (_rms_scale(a, sw_ref[...]) * (1.0 - lambda_init)).astype(o_ref.dtype)


def diff_attention(q, kv, lam_rows, subln_w, *, batch, seq, tq, tk, lambda_init, q_col0):
    t = q.shape[0]
    hw = 2 * DIFF_DH
    nq = seq // tq
    nk = seq // tk
    pairs = [(a, b) for a in range(nq) for b in range(nk) if b * tk < (a + 1) * tq]
    qi_tab = jnp.asarray([a for a, _ in pairs], jnp.int32)
    ki_tab = jnp.asarray([b for _, b in pairs], jnp.int32)
    kern = functools.partial(_attn_kernel, tq=tq, tk=tk, lambda_init=lambda_init)
    grid_spec = pltpu.PrefetchScalarGridSpec(
        num_scalar_prefetch=2,
        grid=(batch, DIFF_HEADS, len(pairs)),
        in_specs=[
            pl.BlockSpec((tq, hw), lambda b, h, p, qt, kt: (b * nq + qt[p], q_col0 + h)),
            pl.BlockSpec((tk, hw), lambda b, h, p, qt, kt: (b * nk + kt[p], h)),
            pl.BlockSpec((tk, hw), lambda b, h, p, qt, kt: (b * nk + kt[p], DIFF_HEADS + h)),
            pl.BlockSpec((SUBLANES, DIFF_DH), lambda b, h, p, qt, kt: (0, 0)),
            pl.BlockSpec((1, hw), lambda b, h, p, qt, kt: (0, 0)),
        ],
        out_specs=pl.BlockSpec((tq, hw), lambda b, h, p, qt, kt: (b * nq + qt[p], h)),
        scratch_shapes=[
            pltpu.VMEM((2, tq, LANES), F32),
            pltpu.VMEM((2, tq, LANES), F32),
            pltpu.VMEM((2, tq, hw), F32),
        ],
    )
    return pl.pallas_call(
        kern,
        grid_spec=grid_spec,
        out_shape=jax.ShapeDtypeStruct((t, DIFF_HEADS * hw), BF16),
        compiler_params=_params("parallel", "parallel", "arbitrary"),
        name="diff_attention",
    )(qi_tab, ki_tab, q, kv, kv, lam_rows, subln_w.reshape(1, hw))


def kernel(x, norm_mix_pre, norm_mix_post, norm_ffn_pre, norm_ffn_post, gdn_w_in, gdn_conv_w, gdn_a_log,
           gdn_dt_bias, gdn_norm_w, gdn_w_out, kv_norm_w, w_kv, diff_w_q, diff_lq1, diff_lk1, diff_lq2,
           diff_lk2, diff_subln_w, diff_w_o, ffn_w_up, ffn_conv_w, ffn_conv_b, ffn_w_down):
    batch, seq, d = x.shape
    t = batch * seq
    h = x.reshape(t, d)
    n_main = 2 * GDN_QK_DIM + 2 * GDN_V_DIM

    def ffn(h, layer):
        return conv_ffn(h, norm_ffn_pre[layer], ffn_w_up[layer].astype(BF16), ffn_conv_w[layer],
                        ffn_conv_b[layer], ffn_w_down[layer].astype(BF16), norm_ffn_post[layer],
                        seq=seq, tm=512, tf=512)

    for layer in range(DEPTH):
        if layer < N_A:
            w_in = gdn_w_in[layer].astype(BF16)
            w_ba = jnp.pad(w_in[:, n_main:], ((0, 0), (0, LANES - 2 * GDN_HEADS)))
            proj, ba = norm_matmul(h, norm_mix_pre[layer][None], w_in, BF16, tm=1024, tn=512, n_out=n_main,
                                   w_extra=w_ba)
            prm = jnp.zeros((SUBLANES, LANES), F32)
            prm = prm.at[0, GDN_HEADS:2 * GDN_HEADS].set(gdn_a_log[layer].astype(F32))
            prm = prm.at[1, GDN_HEADS:2 * GDN_HEADS].set(gdn_dt_bias[layer].astype(F32))
            gates, gates_t = gdn_gates(ba, prm)
            conv_w = gdn_conv_w[layer].reshape(GDN_CONV, 3, GDN_QK_DIM)
            o = gdn_core(proj, gates, gates_t, conv_w, gdn_norm_w[layer], batch=batch, seq=seq, hb=8, ts=1024,
                         chunks_a=1)
            h = matmul_norm_res(o, gdn_w_out[layer].astype(BF16), h, norm_mix_post[layer], tm=512)
        else:
            j = layer - N_A
            lambda_init = 0.8 - 0.6 * math.exp(-0.3 * layer)
            q_scale = DIFF_DH ** -0.5 * math.log2(math.e)
            tn = 512
            if layer == N_A:
                w_kvq = jnp.concatenate([w_kv, diff_w_q[j]], axis=1).astype(BF16)
                kv = norm_matmul(h, jnp.stack([kv_norm_w, norm_mix_pre[layer]]), w_kvq, BF16, tm=1024, tn=tn,
                                 split=w_kv.shape[1] // tn, out_scale=q_scale)
                q, q_col0 = kv, w_kv.shape[1] // (2 * DIFF_DH)
            else:
                q = norm_matmul(h, norm_mix_pre[layer][None], diff_w_q[j].astype(BF16), BF16, tm=1024, tn=tn,
                                out_scale=q_scale)
                q_col0 = 0
            lam_rows = jnp.zeros((SUBLANES, DIFF_DH), F32)
            lam_rows = lam_rows.at[0].set(diff_lq1[j]).at[1].set(diff_lk1[j])
            lam_rows = lam_rows.at[2].set(diff_lq2[j]).at[3].set(diff_lk2[j])
            att = diff_attention(q, kv, lam_rows, diff_subln_w[j], batch=batch, seq=seq, tq=1024, tk=512,
                                 lambda_init=lambda_init, q_col0=q_col0)
            h = matmul_norm_res(att, diff_w_o[j].astype(BF16), h, norm_mix_post[layer], tm=512)
        h = ffn(h, layer)
    return h.reshape(batch, seq, d)
```

```python
import functools
import math

import jax
import jax.numpy as jnp
from jax import lax
from jax.experimental import pallas as pl
from jax.experimental.pallas import tpu as pltpu

D_MODEL = 2048
DEPTH = 2
N_A = DEPTH // 2
CHUNK = 64
GDN_HEADS = 16
GDN_DK = 128
GDN_DV = 128
GDN_CONV = 4
GDN_QK_DIM = GDN_HEADS * GDN_DK
GDN_V_DIM = GDN_HEADS * GDN_DV
DIFF_HEADS = 8
DIFF_DH = 128
DIFF_QK_DIM = DIFF_HEADS * 2 * DIFF_DH
D_FF = 5632
FFN_CONV = 3
EPS = 1e-6

LANES = 128
SUBLANES = 8
VMEM_LIMIT_BYTES = 56 * 1024 * 1024

DELTA_CHUNK = 128
INV_BASE = 16

F32 = jnp.float32
BF16 = jnp.bfloat16
NT_DIMS = (((1,), (1,)), ((), ()))
TN_DIMS = (((0,), (0,)), ((), ()))


def _bdot(a, b):
    return jnp.dot(a.astype(BF16), b.astype(BF16), preferred_element_type=F32)


def _rms_scale(x, gain):
    ms = jnp.mean(x * x, axis=-1, keepdims=True)
    return x * lax.rsqrt(ms + EPS) * gain


def _silu(x):
    return x * jax.nn.sigmoid(x)


def _params(*sem):
    return pltpu.CompilerParams(dimension_semantics=sem, vmem_limit_bytes=VMEM_LIMIT_BYTES)


def _causal_delay(y, prev, shift):
    head = jnp.concatenate([prev, y[0:SUBLANES, :]], axis=0)
    first = pltpu.roll(head, shift, axis=0)[SUBLANES:, :]
    body = pltpu.roll(y, shift, axis=0)
    return jnp.concatenate([first, body[SUBLANES:, :]], axis=0)


def _norm_matmul_kernel(x_ref, g_ref, w_ref, o_ref, xn_ref, *, split, out_scale):
    j = pl.program_id(1)
    nseg = g_ref.shape[0]

    @pl.when(j == 0)
    def _():
        x = x_ref[...]
        xh = x * lax.rsqrt(jnp.mean(x * x, axis=-1, keepdims=True) + EPS)
        for s in range(nseg):
            xn_ref[s] = (xh * g_ref[s:s + 1, :]).astype(BF16)

    if nseg == 1:
        y = jnp.dot(xn_ref[0], w_ref[...], preferred_element_type=F32)
        if out_scale is not None:
            y = y * out_scale
    else:
        second = j >= split
        y = jnp.dot(xn_ref[second.astype(jnp.int32)], w_ref[...], preferred_element_type=F32)
        if out_scale is not None:
            y = y * jnp.where(second, out_scale, 1.0)
    o_ref[...] = y.astype(o_ref.dtype)


def norm_matmul(x, gains, w, out_dtype, *, tm, tn, split=None, out_scale=None):
    m, k = x.shape
    n = w.shape[1]
    nseg = gains.shape[0]
    return pl.pallas_call(
        functools.partial(_norm_matmul_kernel, split=split, out_scale=out_scale),
        grid=(m // tm, n // tn),
        in_specs=[
            pl.BlockSpec((tm, k), lambda i, j: (i, 0)),
            pl.BlockSpec((nseg, k), lambda i, j: (0, 0)),
            pl.BlockSpec((k, tn), lambda i, j: (0, j)),
        ],
        out_specs=pl.BlockSpec((tm, tn), lambda i, j: (i, j)),
        out_shape=jax.ShapeDtypeStruct((m, n), out_dtype),
        scratch_shapes=[pltpu.VMEM((nseg, tm, k), BF16)],
        compiler_params=_params("parallel", "arbitrary"),
        name="norm_matmul",
    )(x, gains, w)


def _gdn_in_proj_kernel(x_ref, g_ref, w_ref, wba_ref, cw_ref, o_ref, ba_ref, xn_ref, carry_ref, *,
                        tm, row_parts, tiles_per_seq, n_q_tiles, n_norm_tiles, n_conv_tiles):
    i = pl.program_id(0)
    j = pl.program_id(1)
    tn = w_ref.shape[1]

    @pl.when(j == 0)
    def _():
        xn_ref[...] = _rms_scale(x_ref[...], g_ref[...]).astype(BF16)
        ba_ref[...] = jnp.dot(xn_ref[...], wba_ref[...], preferred_element_type=F32)

    rp = tm // row_parts

    def project():
        return [jnp.dot(xn_ref[r * rp:(r + 1) * rp, :], w_ref[...], preferred_element_type=F32)
                for r in range(row_parts)]

    def conv_silu(y, prev):
        acc = None
        for tap in range(GDN_CONV):
            delay = GDN_CONV - 1 - tap
            yd = y if delay == 0 else _causal_delay(y, prev, delay)
            term = yd * cw_ref[tap:tap + 1, :]
            acc = term if acc is None else acc + term
        return _silu(acc)

    def epilogue(normalize):
        ys = project()
        seq_start = (i % tiles_per_seq) == 0
        prev = jnp.where(seq_start, 0.0, carry_ref[j])
        carry_ref[j] = ys[-1][rp - SUBLANES:, :]
        scale = jnp.where(j < n_q_tiles, GDN_DK ** -0.5, 1.0)
        for r, y in enumerate(ys):
            act = conv_silu(y, prev)
            prev = y[rp - SUBLANES:, :]
            if normalize:
                heads = []
                for c0 in range(0, tn, GDN_DK):
                    a = act[:, c0:c0 + GDN_DK]
                    heads.append(a * (lax.rsqrt(jnp.sum(a * a, axis=-1, keepdims=True) + EPS) * scale))
                act = jnp.concatenate(heads, axis=1)
            o_ref[r * rp:(r + 1) * rp, :] = act.astype(o_ref.dtype)

    pl.when(j < n_norm_tiles)(functools.partial(epilogue, True))
    pl.when(jnp.logical_and(j >= n_norm_tiles, j < n_conv_tiles))(functools.partial(epilogue, False))

    @pl.when(j >= n_conv_tiles)
    def _():
        for r, y in enumerate(project()):
            o_ref[r * rp:(r + 1) * rp, :] = y.astype(o_ref.dtype)


def gdn_in_proj(x, gain, w_in, w_ba, conv_w, *, seq, tm, tn, row_parts):
    m, k = x.shape
    n_main = 2 * GDN_QK_DIM + 2 * GDN_V_DIM
    n_conv_tiles = (2 * GDN_QK_DIM + GDN_V_DIM) // tn
    kern = functools.partial(
        _gdn_in_proj_kernel, tm=tm, row_parts=row_parts, tiles_per_seq=seq // tm, n_q_tiles=GDN_QK_DIM // tn,
        n_norm_tiles=2 * GDN_QK_DIM // tn, n_conv_tiles=n_conv_tiles)
    return pl.pallas_call(
        kern,
        grid=(m // tm, n_main // tn),
        in_specs=[
            pl.BlockSpec((tm, k), lambda i, j: (i, 0)),
            pl.BlockSpec((1, k), lambda i, j: (0, 0)),
            pl.BlockSpec((k, tn), lambda i, j: (0, j)),
            pl.BlockSpec((k, LANES), lambda i, j: (0, 0)),
            pl.BlockSpec((GDN_CONV, tn), lambda i, j: (0, jnp.minimum(j, n_conv_tiles - 1))),
        ],
        out_specs=[
            pl.BlockSpec((tm, tn), lambda i, j: (i, j)),
            pl.BlockSpec((tm, LANES), lambda i, j: (i, 0)),
        ],
        out_shape=[
            jax.ShapeDtypeStruct((m, n_main), BF16),
            jax.ShapeDtypeStruct((m, LANES), F32),
        ],
        scratch_shapes=[
            pltpu.VMEM((tm, k), BF16),
            pltpu.VMEM((n_conv_tiles, SUBLANES, tn), F32),
        ],
        compiler_params=_params("arbitrary", "arbitrary"),
        name="gdn_in_proj",
    )(x, gain.reshape(1, k), w_in, w_ba, conv_w)


def _matmul_norm_res_kernel(a_ref, w_ref, h_ref, g_ref, o_ref):
    m = jnp.dot(a_ref[...], w_ref[...], preferred_element_type=F32)
    o_ref[...] = h_ref[...] + _rms_scale(m, g_ref[...])


def matmul_norm_res(a, w, h, gain, *, tm):
    m, k = a.shape
    n = w.shape[1]
    return pl.pallas_call(
        _matmul_norm_res_kernel,
        grid=(m // tm,),
        in_specs=[
            pl.BlockSpec((tm, k), lambda i: (i, 0)),
            pl.BlockSpec((k, n), lambda i: (0, 0)),
            pl.BlockSpec((tm, n), lambda i: (i, 0)),
            pl.BlockSpec((1, n), lambda i: (0, 0)),
        ],
        out_specs=pl.BlockSpec((tm, n), lambda i: (i, 0)),
        out_shape=jax.ShapeDtypeStruct((m, n), F32),
        compiler_params=_params("parallel"),
        name="matmul_norm_res",
    )(a, w, h, gain.reshape(1, n))


def _ffn_kernel(h_ref, gpre_ref, wg_ref, wu_ref, cw_ref, cb_ref, wd_ref, gpost_ref, o_ref,
                xn_ref, acc_ref, carry_ref, *, tm, tiles_per_seq):
    i = pl.program_id(0)
    f = pl.program_id(1)

    @pl.when(f == 0)
    def _():
        xn_ref[...] = _rms_scale(h_ref[...], gpre_ref[...]).astype(BF16)
        acc_ref[...] = jnp.zeros_like(acc_ref)

    xn = xn_ref[...]
    gate = jnp.dot(xn, wg_ref[...], preferred_element_type=F32)
    up = jnp.dot(xn, wu_ref[...], preferred_element_type=F32)

    seq_start = (i % tiles_per_seq) == 0
    prev = jnp.where(seq_start, 0.0, carry_ref[f])
    carry_ref[f] = gate[tm - SUBLANES:, :]
    conv = (_causal_delay(gate, prev, 2) * cw_ref[0:1, :] + _causal_delay(gate, prev, 1) * cw_ref[1:2, :]
            + gate * cw_ref[2:3, :] + cb_ref[...])
    act = (_silu(conv) * up).astype(BF16)
    acc_ref[...] += jnp.dot(act, wd_ref[...], preferred_element_type=F32)

    @pl.when(f == pl.num_programs(1) - 1)
    def _():
        o_ref[...] = h_ref[...] + _rms_scale(acc_ref[...], gpost_ref[...])


def conv_ffn(h, gpre, w_up, conv_w, conv_b, w_down, gpost, *, layer, seq, tm, tf):
    m, d = h.shape
    dff = w_down.shape[1]
    nf = dff // tf
    kern = functools.partial(_ffn_kernel, tm=tm, tiles_per_seq=seq // tm)
    return pl.pallas_call(
        kern,
        grid=(m // tm, nf),
        in_specs=[
            pl.BlockSpec((tm, d), lambda i, f: (i, 0)),
            pl.BlockSpec((1, d), lambda i, f: (0, 0)),
            pl.BlockSpec((None, d, tf), lambda i, f: (layer, 0, f)),
            pl.BlockSpec((None, d, tf), lambda i, f: (layer, 0, f + nf)),
            pl.BlockSpec((FFN_CONV, tf), lambda i, f: (0, f)),
            pl.BlockSpec((1, tf), lambda i, f: (0, f)),
            pl.BlockSpec((None, tf, d), lambda i, f: (layer, f, 0)),
            pl.BlockSpec((1, d), lambda i, f: (0, 0)),
        ],
        out_specs=pl.BlockSpec((tm, d), lambda i, f: (i, 0)),
        out_shape=jax.ShapeDtypeStruct((m, d), F32),
        scratch_shapes=[
            pltpu.VMEM((tm, d), BF16),
            pltpu.VMEM((tm, d), F32),
            pltpu.VMEM((nf, SUBLANES, tf), F32),
        ],
        compiler_params=_params("arbitrary", "arbitrary"),
        name="conv_ffn",
    )(h, gpre.reshape(1, d), w_up, w_up, conv_w, conv_b.reshape(1, dff), w_down, gpost.reshape(1, d))


def _gates_kernel(ba_ref, prm_ref, g_ref, gt_ref):
    c = DELTA_CHUNK
    lane = lax.broadcasted_iota(jnp.int32, (c, LANES), 1)
    is_beta = lane < GDN_HEADS
    is_g = jnp.logical_and(lane >= GDN_HEADS, lane < 2 * GDN_HEADS)
    row = lax.broadcasted_iota(jnp.int32, (c, c), 0)
    col = lax.broadcasted_iota(jnp.int32, (c, c), 1)
    tril = (row >= col).astype(F32)
    for ci in range(gt_ref.shape[0]):
        ba = ba_ref[ci * c:(ci + 1) * c, :]
        g = -jnp.exp(prm_ref[0:1, :]) * jax.nn.softplus(ba + prm_ref[1:2, :])
        g = jnp.where(is_g, g, 0.0)
        gcum = jnp.dot(tril, g, preferred_element_type=F32, precision=lax.Precision.HIGHEST)
        out = jnp.where(is_beta, jax.nn.sigmoid(ba), gcum)
        g_ref[ci * c:(ci + 1) * c, :] = out
        gt_ref[ci] = out.T


def gdn_gates(ba, prm, *, chunks_per_step):
    t = ba.shape[0]
    c = DELTA_CHUNK
    rows = chunks_per_step * c
    return pl.pallas_call(
        _gates_kernel,
        grid=(t // rows,),
        in_specs=[
            pl.BlockSpec((rows, LANES), lambda i: (i, 0)),
            pl.BlockSpec((SUBLANES, LANES), lambda i: (0, 0)),
        ],
        out_specs=[
            pl.BlockSpec((rows, LANES), lambda i: (i, 0)),
            pl.BlockSpec((chunks_per_step, LANES, c), lambda i: (i, 0, 0)),
        ],
        out_shape=[
            jax.ShapeDtypeStruct((t, LANES), F32),
            jax.ShapeDtypeStruct((t // c, LANES, c), F32),
        ],
        compiler_params=_params("parallel"),
        name="gdn_gates",
    )(ba, prm)


def _unit_lower_inverse(lows, masks):
    eye, base_mask, merge_masks = masks
    a_s = [jnp.where(base_mask, -low, 0.0) for low in lows]
    xs = [eye + a for a in a_s]
    span = 2
    while span < INV_BASE:
        a_s = [_bdot(a, a) for a in a_s]
        xs = [x + _bdot(x, a) for x, a in zip(xs, a_s)]
        span *= 2
    for mm in merge_masks:
        ys = [_bdot(x, jnp.where(mm, low, 0.0)) for x, low in zip(xs, lows)]
        xs = [x - _bdot(y, x) for x, y in zip(xs, ys)]
    return xs


def _inverse_masks(c):
    row = lax.broadcasted_iota(jnp.int32, (c, c), 0)
    col = lax.broadcasted_iota(jnp.int32, (c, c), 1)
    eye = (row == col).astype(F32)
    same = lambda b: (row // b) == (col // b)
    base_mask = same(INV_BASE)
    merge_masks = []
    b = INV_BASE
    while b < c:
        merge_masks.append(jnp.logical_and(same(2 * b), jnp.logical_not(same(b))))
        b *= 2
    return eye, base_mask, merge_masks


def _gdn_kernel(q_ref, k_ref, v_ref, z_ref, g_ref, gt_ref, nw_ref, o_ref,
                state_ref, u_ref, wq_ref, kd_ref, ai_ref, eg_ref, *, hb, ts, chunks_a):
    c = DELTA_CHUNK
    nchunk = ts // c
    hg = pl.program_id(1)
    tb = pl.program_id(2)

    @pl.when(tb == 0)
    def _():
        state_ref[...] = jnp.zeros_like(state_ref)

    row = lax.broadcasted_iota(jnp.int32, (c, c), 0)
    col = lax.broadcasted_iota(jnp.int32, (c, c), 1)
    causal = row >= col
    strict = row > col
    inv_masks = _inverse_masks(c)
    lane = lax.broadcasted_iota(jnp.int32, (c, LANES), 1)

    def phase_a(it, carry):
        items = []
        for uu in range(chunks_a):
            ci = it * chunks_a + uu
            r0 = pl.multiple_of(ci * c, c)
            gates = g_ref[pl.ds(r0, c), :]
            for i in range(hb):
                head = hg * hb + i
                lanes = slice(i * LANES, (i + 1) * LANES)
                beta = jnp.sum(jnp.where(lane == head, gates, 0.0), axis=-1, keepdims=True)
                gc = jnp.sum(jnp.where(lane == head + GDN_HEADS, gates, 0.0), axis=-1, keepdims=True)
                gc_row = gt_ref[ci, pl.ds(head + GDN_HEADS, 1), :]
                items.append(dict(ci=ci, r0=r0, i=i, lanes=lanes, beta=beta, gc=gc, gc_row=gc_row,
                                  g_last=gc_row[:, c - 1:c]))
        for d in items:
            d["qn"] = q_ref[pl.ds(d["r0"], c), d["lanes"]]
            d["kn"] = k_ref[pl.ds(d["r0"], c), d["lanes"]]
            d["kb"] = d["kn"].astype(F32) * d["beta"]
        for d in items:
            d["qk"] = lax.dot_general(jnp.concatenate([d["qn"], d["kb"].astype(BF16)], axis=0), d["kn"],
                                      NT_DIMS, preferred_element_type=F32)
        lows = []
        for d in items:
            decay = jnp.exp(jnp.where(causal, d["gc"] - d["gc_row"], -jnp.inf))
            ai_ref[pl.ds(d["r0"], c), d["lanes"]] = jnp.where(causal, d["qk"][:c] * decay, 0.0).astype(BF16)
            lows.append(jnp.where(strict, d["qk"][c:] * decay, 0.0))
        tinvs = _unit_lower_inverse(lows, inv_masks)
        rhss = []
        for d in items:
            d["egc"] = jnp.exp(d["gc"])
            vv = v_ref[pl.ds(d["r0"], c), d["lanes"]].astype(F32)
            rhss.append(jnp.concatenate([vv * d["beta"], d["kb"] * d["egc"]], axis=1))
        sols = [_bdot(tinv, rhs) for tinv, rhs in zip(tinvs, rhss)]
        for d, sol in zip(items, sols):
            r0, lanes = d["r0"], d["lanes"]
            u_ref[pl.ds(r0, c), lanes] = sol[:, :LANES]
            wq_ref[0, pl.ds(r0, c), lanes] = sol[:, LANES:].astype(BF16)
            wq_ref[1, pl.ds(r0, c), lanes] = (d["qn"].astype(F32) * d["egc"]).astype(BF16)
            kd_ref[pl.ds(r0, c), lanes] = (d["kn"].astype(F32) * jnp.exp(d["g_last"] - d["gc"])).astype(BF16)
            eg_ref[d["ci"], d["i"]] = jnp.broadcast_to(jnp.exp(d["g_last"]), (SUBLANES, LANES))
        return carry

    lax.fori_loop(0, nchunk // chunks_a, phase_a, 0)

    def phase_b(ci, carry):
        r0 = pl.multiple_of(ci * c, c)
        heads = range(hb)
        lanes = [slice(i * LANES, (i + 1) * LANES) for i in heads]
        states = [state_ref[i] for i in heads]
        wss = [jnp.dot(jnp.concatenate([wq_ref[0, pl.ds(r0, c), lanes[i]], wq_ref[1, pl.ds(r0, c), lanes[i]]],
                                       axis=0),
                       states[i].astype(BF16), preferred_element_type=F32) for i in heads]
        v_news = [(u_ref[pl.ds(r0, c), lanes[i]] - wss[i][:c]).astype(BF16) for i in heads]
        upds = [lax.dot_general(kd_ref[pl.ds(r0, c), lanes[i]], v_news[i], TN_DIMS, preferred_element_type=F32)
                for i in heads]
        for i in heads:
            state_ref[i] = states[i] * eg_ref[ci, i][0:1, :] + upds[i]
        outs = [wss[i][c:] + jnp.dot(ai_ref[pl.ds(r0, c), lanes[i]], v_news[i], preferred_element_type=F32)
                for i in heads]
        for i in heads:
            z = z_ref[pl.ds(r0, c), lanes[i]].astype(F32)
            o_ref[pl.ds(r0, c), lanes[i]] = (_rms_scale(outs[i], nw_ref[...]) * _silu(z)).astype(o_ref.dtype)
        return carry

    lax.fori_loop(0, nchunk, phase_b, 0)


def gdn_core(proj, gates, gates_t, norm_w, *, batch, seq, hb, ts, chunks_a):
    t = proj.shape[0]
    w = hb * LANES
    ngroups = GDN_HEADS // hb
    nt = seq // ts
    nchunk = ts // DELTA_CHUNK
    kern = functools.partial(_gdn_kernel, hb=hb, ts=ts, chunks_a=chunks_a)

    def col_spec(base):
        return pl.BlockSpec((ts, w), lambda b, h, s: (b * nt + s, base * ngroups + h))

    return pl.pallas_call(
        kern,
        grid=(batch, ngroups, nt),
        in_specs=[
            col_spec(0), col_spec(1), col_spec(2), col_spec(3),
            pl.BlockSpec((ts, LANES), lambda b, h, s: (b * nt + s, 0)),
            pl.BlockSpec((nchunk, LANES, DELTA_CHUNK), lambda b, h, s: (b * nt + s, 0, 0)),
            pl.BlockSpec((1, LANES), lambda b, h, s: (0, 0)),
        ],
        out_specs=pl.BlockSpec((ts, w), lambda b, h, s: (b * nt + s, h)),
        out_shape=jax.ShapeDtypeStruct((t, GDN_V_DIM), BF16),
        scratch_shapes=[
            pltpu.VMEM((hb, GDN_DK, GDN_DV), F32),
            pltpu.VMEM((ts, w), F32),
            pltpu.VMEM((2, ts, w), BF16),
            pltpu.VMEM((ts, w), BF16),
            pltpu.VMEM((ts, w), BF16),
            pltpu.VMEM((nchunk, hb, SUBLANES, LANES), F32),
        ],
        compiler_params=_params("parallel", "parallel", "arbitrary"),
        name="gdn_core",
    )(proj, proj, proj, proj, gates, gates_t, norm_w.reshape(1, LANES))


def _attn_kernel(qi_tab, ki_tab, q_ref, k_ref, v_ref, lam_ref, sw_ref, o_ref,
                 m_ref, l_ref, acc_ref, *, tq, tk, lambda_init):
    p = pl.program_id(2)
    qi = qi_tab[p]
    ki = ki_tab[p]
    dh = DIFF_DH

    @pl.when(ki == 0)
    def _():
        m_ref[...] = jnp.full_like(m_ref, -jnp.inf)
        l_ref[...] = jnp.zeros_like(l_ref)
        acc_ref[...] = jnp.zeros_like(acc_ref)

    parts = tq // tk
    row = lax.broadcasted_iota(jnp.int32, (tk, tk), 0)
    col = lax.broadcasted_iota(jnp.int32, (tk, tk), 1)
    diag_visible = (col // CHUNK) <= (row // CHUNK)

    def update(diag_part):
        k = k_ref[...]
        v = v_ref[...]
        first = 0 if diag_part is None else diag_part
        chains = [(r, m) for r in range(first, parts) for m in range(2)]

        def scores(r, m):
            return lax.dot_general(q_ref[r * tk:(r + 1) * tk, m * dh:(m + 1) * dh], k[:, m * dh:(m + 1) * dh],
                                   NT_DIMS, preferred_element_type=F32)

        s_next = scores(*chains[0])
        for idx, (r, m) in enumerate(chains):
            s = s_next
            if idx + 1 < len(chains):
                s_next = scores(*chains[idx + 1])
            rows = slice(r * tk, (r + 1) * tk)
            if r == diag_part:
                s = jnp.where(diag_visible, s, -jnp.inf)
            m_old = m_ref[m, rows, :]
            m_new = jnp.maximum(m_old, jnp.max(s, axis=-1, keepdims=True))
            alpha = jnp.exp2(m_old - m_new)
            pr = jnp.exp2(s - jnp.concatenate([m_new] * (tk // LANES), axis=1))
            psum = pr[:, 0:LANES]
            for t in range(1, tk // LANES):
                psum = psum + pr[:, t * LANES:(t + 1) * LANES]
            l_ref[m, rows, :] = alpha * l_ref[m, rows, :] + psum
            acc_ref[m, rows, :] = (jnp.concatenate([alpha] * (2 * dh // LANES), axis=1) * acc_ref[m, rows, :]
                                   + jnp.dot(pr.astype(BF16), v, preferred_element_type=F32))
            m_ref[m, rows, :] = m_new

    d = ki - qi * parts
    pl.when(d < 0)(functools.partial(update, None))
    for j in range(parts):
        pl.when(d == j)(functools.partial(update, j))

    @pl.when(d == parts - 1)
    def _():
        lam = (jnp.exp(jnp.sum(lam_ref[0:1, :] * lam_ref[1:2, :], axis=-1, keepdims=True))
               - jnp.exp(jnp.sum(lam_ref[2:3, :] * lam_ref[3:4, :], axis=-1, keepdims=True))
               + lambda_init)
        l0 = jnp.sum(l_ref[0], axis=-1, keepdims=True)
        l1 = jnp.sum(l_ref[1], axis=-1, keepdims=True)
        a = acc_ref[0] / l0 - lam * (acc_ref[1] / l1)
        o_ref[...] = (_rms_scale(a, sw_ref[...]) * (1.0 - lambda_init)).astype(o_ref.dtype)


def diff_attention(q, kv, lam_rows, subln_w, *, batch, seq, tq, tk, lambda_init, q_col0):
    t = q.shape[0]
    hw = 2 * DIFF_DH
    nq = seq // tq
    nk = seq // tk
    pairs = [(a, b) for a in range(nq) for b in range(nk) if b * tk < (a + 1) * tq]
    qi_tab = jnp.asarray([a for a, _ in pairs], jnp.int32)
    ki_tab = jnp.asarray([b for _, b in pairs], jnp.int32)
    kern = functools.partial(_attn_kernel, tq=tq, tk=tk, lambda_init=lambda_init)
    grid_spec = pltpu.PrefetchScalarGridSpec(
        num_scalar_prefetch=2,
        grid=(batch, DIFF_HEADS, len(pairs)),
        in_specs=[
            pl.BlockSpec((tq, hw), lambda b, h, p, qt, kt: (b * nq + qt[p], q_col0 + h)),
            pl.BlockSpec((tk, hw), lambda b, h, p, qt, kt: (b * nk + kt[p], h)),
            pl.BlockSpec((tk, hw), lambda b, h, p, qt, kt: (b * nk + kt[p], DIFF_HEADS + h)),
            pl.BlockSpec((SUBLANES, DIFF_DH), lambda b, h, p, qt, kt: (0, 0)),
            pl.BlockSpec((1, hw), lambda b, h, p, qt, kt: (0, 0)),
        ],
        out_specs=pl.BlockSpec((tq, hw), lambda b, h, p, qt, kt: (b * nq + qt[p], h)),
        scratch_shapes=[
            pltpu.VMEM((2, tq, LANES), F32),
            pltpu.VMEM((2, tq, LANES), F32),
            pltpu.VMEM((2, tq, hw), F32),
        ],
    )
    return pl.pallas_call(
        kern,
        grid_spec=grid_spec,
        out_shape=jax.ShapeDtypeStruct((t, DIFF_HEADS * hw), BF16),
        compiler_params=_params("parallel", "parallel", "arbitrary"),
        name="diff_attention",
    )(qi_tab, ki_tab, q, kv, kv, lam_rows, subln_w.reshape(1, hw))


def kernel(x, norm_mix_pre, norm_mix_post, norm_ffn_pre, norm_ffn_post, gdn_w_in, gdn_conv_w, gdn_a_log,
           gdn_dt_bias, gdn_norm_w, gdn_w_out, kv_norm_w, w_kv, diff_w_q, diff_lq1, diff_lk1, diff_lq2,
           diff_lk2, diff_subln_w, diff_w_o, ffn_w_up, ffn_conv_w, ffn_conv_b, ffn_w_down):
    batch, seq, d = x.shape
    t = batch * seq
    h = x.reshape(t, d)
    n_main = 2 * GDN_QK_DIM + 2 * GDN_V_DIM
    ffn_w_up_b = ffn_w_up.astype(BF16)
    ffn_w_down_b = ffn_w_down.astype(BF16)

    for layer in range(DEPTH):
        if layer < N_A:
            w_in = gdn_w_in[layer].astype(BF16)
            w_ba = jnp.pad(w_in[:, n_main:], ((0, 0), (0, LANES - 2 * GDN_HEADS)))
            proj, ba = gdn_in_proj(h, norm_mix_pre[layer], w_in, w_ba, gdn_conv_w[layer], seq=seq, tm=1024,
                                   tn=512, row_parts=4)
            prm = jnp.zeros((SUBLANES, LANES), F32)
            prm = prm.at[0, GDN_HEADS:2 * GDN_HEADS].set(gdn_a_log[layer].astype(F32))
            prm = prm.at[1, GDN_HEADS:2 * GDN_HEADS].set(gdn_dt_bias[layer].astype(F32))
            gates, gates_t = gdn_gates(ba, prm, chunks_per_step=4)
            o = gdn_core(proj, gates, gates_t, gdn_norm_w[layer], batch=batch, seq=seq, hb=8, ts=1024,
                         chunks_a=1)
            h = matmul_norm_res(o, gdn_w_out[layer].astype(BF16), h, norm_mix_post[layer], tm=512)
        else:
            j = layer - N_A
            lambda_init = 0.8 - 0.6 * math.exp(-0.3 * layer)
            q_scale = DIFF_DH ** -0.5 * math.log2(math.e)
            tn = 512
            if layer == N_A:
                w_kvq = jnp.concatenate([w_kv, diff_w_q[j]], axis=1).astype(BF16)
                kv = norm_matmul(h, jnp.stack([kv_norm_w, norm_mix_pre[layer]]), w_kvq, BF16, tm=1024, tn=tn,
                                 split=w_kv.shape[1] // tn, out_scale=q_scale)
                q, q_col0 = kv, w_kv.shape[1] // (2 * DIFF_DH)
            else:
                q = norm_matmul(h, norm_mix_pre[layer][None], diff_w_q[j].astype(BF16), BF16, tm=1024, tn=tn,
                                out_scale=q_scale)
                q_col0 = 0
            lam_rows = jnp.zeros((SUBLANES, DIFF_DH), F32)
            lam_rows = lam_rows.at[0].set(diff_lq1[j]).at[1].set(diff_lk1[j])
            lam_rows = lam_rows.at[2].set(diff_lq2[j]).at[3].set(diff_lk2[j])
            att = diff_attention(q, kv, lam_rows, diff_subln_w[j], batch=batch, seq=seq, tq=1024, tk=512,
                                 lambda_init=lambda_init, q_col0=q_col0)
            h = matmul_norm_res(att, diff_w_o[j].astype(BF16), h, norm_mix_post[layer], tm=512)
        h = conv_ffn(h, norm_ffn_pre[layer], ffn_w_up_b, ffn_conv_w[layer], ffn_conv_b[layer], ffn_w_down_b,
                     norm_ffn_post[layer], layer=layer, seq=seq, tm=512, tf=512)
    return h.reshape(batch, seq, d)
```

```python
import functools
import math

import jax
import jax.numpy as jnp
from jax import lax
from jax.experimental import pallas as pl
from jax.experimental.pallas import tpu as pltpu

D_MODEL = 2048
DEPTH = 2
N_A = DEPTH // 2
CHUNK = 64
GDN_HEADS = 16
GDN_DK = 128
GDN_DV = 128
GDN_CONV = 4
GDN_QK_DIM = GDN_HEADS * GDN_DK
GDN_V_DIM = GDN_HEADS * GDN_DV
DIFF_HEADS = 8
DIFF_DH = 128
DIFF_QK_DIM = DIFF_HEADS * 2 * DIFF_DH
D_FF = 5632
FFN_CONV = 3
EPS = 1e-6

LANES = 128
SUBLANES = 8
STRIP = 128
VMEM_LIMIT_BYTES = 56 * 1024 * 1024

DELTA_CHUNK = 128
INV_BASE = 16

F32 = jnp.float32
BF16 = jnp.bfloat16
NT_DIMS = (((1,), (1,)), ((), ()))
TN_DIMS = (((0,), (0,)), ((), ()))


def _bdot(a, b):
    return jnp.dot(a.astype(BF16), b.astype(BF16), preferred_element_type=F32)


def _rms_scale(x, gain):
    ms = jnp.mean(x * x, axis=-1, keepdims=True)
    return x * lax.rsqrt(ms + EPS) * gain


def _silu(x):
    return x * jax.nn.sigmoid(x)


def _params(*sem):
    return pltpu.CompilerParams(dimension_semantics=sem, vmem_limit_bytes=VMEM_LIMIT_BYTES)


def _causal_delay(y, prev, shift):
    head = jnp.concatenate([prev, y[0:SUBLANES, :]], axis=0)
    first = pltpu.roll(head, shift, axis=0)[SUBLANES:, :]
    body = pltpu.roll(y, shift, axis=0)
    return jnp.concatenate([first, body[SUBLANES:, :]], axis=0)


def _norm_matmul_kernel(x_ref, g_ref, w_ref, o_ref, xn_ref, *, split, out_scale):
    j = pl.program_id(1)
    nseg = g_ref.shape[0]

    @pl.when(j == 0)
    def _():
        x = x_ref[...]
        xh = x * lax.rsqrt(jnp.mean(x * x, axis=-1, keepdims=True) + EPS)
        for s in range(nseg):
            xn_ref[s] = (xh * g_ref[s:s + 1, :]).astype(BF16)

    if nseg == 1:
        y = jnp.dot(xn_ref[0], w_ref[...], preferred_element_type=F32)
        if out_scale is not None:
            y = y * out_scale
    else:
        second = j >= split
        y = jnp.dot(xn_ref[second.astype(jnp.int32)], w_ref[...], preferred_element_type=F32)
        if out_scale is not None:
            y = y * jnp.where(second, out_scale, 1.0)
    o_ref[...] = y.astype(o_ref.dtype)


def norm_matmul(x, gains, w, out_dtype, *, tm, tn, split=None, out_scale=None):
    m, k = x.shape
    n = w.shape[1]
    nseg = gains.shape[0]
    return pl.pallas_call(
        functools.partial(_norm_matmul_kernel, split=split, out_scale=out_scale),
        grid=(m // tm, n // tn),
        in_specs=[
            pl.BlockSpec((tm, k), lambda i, j: (i, 0)),
            pl.BlockSpec((nseg, k), lambda i, j: (0, 0)),
            pl.BlockSpec((k, tn), lambda i, j: (0, j)),
        ],
        out_specs=pl.BlockSpec((tm, tn), lambda i, j: (i, j)),
        out_shape=jax.ShapeDtypeStruct((m, n), out_dtype),
        scratch_shapes=[pltpu.VMEM((nseg, tm, k), BF16)],
        compiler_params=_params("parallel", "arbitrary"),
        name="norm_matmul",
    )(x, gains, w)


def _gdn_in_proj_kernel(x_ref, g_ref, w_ref, wba_ref, cw_ref, o_ref, ba_ref, xn_ref, carry_ref, *,
                        tm, row_parts, tiles_per_seq, n_q_tiles, n_norm_tiles, n_conv_tiles):
    i = pl.program_id(0)
    j = pl.program_id(1)
    tn = w_ref.shape[1]

    @pl.when(j == 0)
    def _():
        xn_ref[...] = _rms_scale(x_ref[...], g_ref[...]).astype(BF16)
        ba_ref[...] = jnp.dot(xn_ref[...], wba_ref[...], preferred_element_type=F32)

    rp = tm // row_parts

    def project():
        return [jnp.dot(xn_ref[r * rp:(r + 1) * rp, :], w_ref[...], preferred_element_type=F32)
                for r in range(row_parts)]

    def conv_silu(y, prev):
        acc = None
        for tap in range(GDN_CONV):
            delay = GDN_CONV - 1 - tap
            yd = y if delay == 0 else _causal_delay(y, prev, delay)
            term = yd * cw_ref[tap:tap + 1, :]
            acc = term if acc is None else acc + term
        return _silu(acc)

    def epilogue(normalize):
        ys = project()
        seq_start = (i % tiles_per_seq) == 0
        prev = jnp.where(seq_start, 0.0, carry_ref[j])
        carry_ref[j] = ys[-1][rp - SUBLANES:, :]
        scale = jnp.where(j < n_q_tiles, GDN_DK ** -0.5, 1.0)
        for r, y in enumerate(ys):
            act = conv_silu(y, prev)
            prev = y[rp - SUBLANES:, :]
            if normalize:
                heads = []
                for c0 in range(0, tn, GDN_DK):
                    a = act[:, c0:c0 + GDN_DK]
                    heads.append(a * (lax.rsqrt(jnp.sum(a * a, axis=-1, keepdims=True) + EPS) * scale))
                act = jnp.concatenate(heads, axis=1)
            o_ref[r * rp:(r + 1) * rp, :] = act.astype(o_ref.dtype)

    pl.when(j < n_norm_tiles)(functools.partial(epilogue, True))
    pl.when(jnp.logical_and(j >= n_norm_tiles, j < n_conv_tiles))(functools.partial(epilogue, False))

    @pl.when(j >= n_conv_tiles)
    def _():
        for r, y in enumerate(project()):
            o_ref[r * rp:(r + 1) * rp, :] = y.astype(o_ref.dtype)


def gdn_in_proj(x, gain, w_in, w_ba, conv_w, *, seq, tm, tn, row_parts):
    m, k = x.shape
    n_main = 2 * GDN_QK_DIM + 2 * GDN_V_DIM
    n_conv_tiles = (2 * GDN_QK_DIM + GDN_V_DIM) // tn
    kern = functools.partial(
        _gdn_in_proj_kernel, tm=tm, row_parts=row_parts, tiles_per_seq=seq // tm, n_q_tiles=GDN_QK_DIM // tn,
        n_norm_tiles=2 * GDN_QK_DIM // tn, n_conv_tiles=n_conv_tiles)
    return pl.pallas_call(
        kern,
        grid=(m // tm, n_main // tn),
        in_specs=[
            pl.BlockSpec((tm, k), lambda i, j: (i, 0)),
            pl.BlockSpec((1, k), lambda i, j: (0, 0)),
            pl.BlockSpec((k, tn), lambda i, j: (0, j)),
            pl.BlockSpec((k, LANES), lambda i, j: (0, 0)),
            pl.BlockSpec((GDN_CONV, tn), lambda i, j: (0, jnp.minimum(j, n_conv_tiles - 1))),
        ],
        out_specs=[
            pl.BlockSpec((tm, tn), lambda i, j: (i, j)),
            pl.BlockSpec((tm, LANES), lambda i, j: (i, 0)),
        ],
        out_shape=[
            jax.ShapeDtypeStruct((m, n_main), BF16),
            jax.ShapeDtypeStruct((m, LANES), F32),
        ],
        scratch_shapes=[
            pltpu.VMEM((tm, k), BF16),
            pltpu.VMEM((n_conv_tiles, SUBLANES, tn), F32),
        ],
        compiler_params=_params("arbitrary", "arbitrary"),
        name="gdn_in_proj",
    )(x, gain.reshape(1, k), w_in, w_ba, conv_w)


def _matmul_norm_res_kernel(a_ref, w_ref, h_ref, g_ref, o_ref):
    m = jnp.dot(a_ref[...], w_ref[...], preferred_element_type=F32)
    o_ref[...] = h_ref[...] + _rms_scale(m, g_ref[...])


def matmul_norm_res(a, w, h, gain, *, tm):
    m, k = a.shape
    n = w.shape[1]
    return pl.pallas_call(
        _matmul_norm_res_kernel,
        grid=(m // tm,),
        in_specs=[
            pl.BlockSpec((tm, k), lambda i: (i, 0)),
            pl.BlockSpec((k, n), lambda i: (0, 0)),
            pl.BlockSpec((tm, n), lambda i: (i, 0)),
            pl.BlockSpec((1, n), lambda i: (0, 0)),
        ],
        out_specs=pl.BlockSpec((tm, n), lambda i: (i, 0)),
        out_shape=jax.ShapeDtypeStruct((m, n), F32),
        compiler_params=_params("parallel"),
        name="matmul_norm_res",
    )(a, w, h, gain.reshape(1, n))


def _ffn_kernel(h_ref, gpre_ref, wg_ref, wu_ref, cw_ref, cb_ref, wd_ref, gpost_ref, o_ref,
                xn_ref, acc_ref, carry_ref, *, tm, row_parts, tiles_per_seq):
    i = pl.program_id(0)
    f = pl.program_id(1)

    @pl.when(f == 0)
    def _():
        xn_ref[...] = _rms_scale(h_ref[...], gpre_ref[...]).astype(BF16)
        acc_ref[...] = jnp.zeros_like(acc_ref)

    rp = tm // row_parts
    parts = [(jnp.dot(xn_ref[r * rp:(r + 1) * rp, :], wg_ref[...], preferred_element_type=F32),
              jnp.dot(xn_ref[r * rp:(r + 1) * rp, :], wu_ref[...], preferred_element_type=F32))
             for r in range(row_parts)]

    seq_start = (i % tiles_per_seq) == 0
    prev = jnp.where(seq_start, 0.0, carry_ref[f])
    carry_ref[f] = parts[-1][0][rp - SUBLANES:, :]
    for r, (gate, up) in enumerate(parts):
        conv = (_causal_delay(gate, prev, 2) * cw_ref[0:1, :] + _causal_delay(gate, prev, 1) * cw_ref[1:2, :]
                + gate * cw_ref[2:3, :] + cb_ref[...])
        prev = gate[rp - SUBLANES:, :]
        act = (_silu(conv) * up).astype(BF16)
        acc_ref[r * rp:(r + 1) * rp, :] += jnp.dot(act, wd_ref[...], preferred_element_type=F32)

    @pl.when(f == pl.num_programs(1) - 1)
    def _():
        o_ref[...] = h_ref[...] + _rms_scale(acc_ref[...], gpost_ref[...])


def conv_ffn(h, gpre, w_up, conv_w, conv_b, w_down, gpost, *, layer, seq, tm, tf):
    m, d = h.shape
    dff = w_down.shape[1]
    nf = dff // tf
    kern = functools.partial(_ffn_kernel, tm=tm, row_parts=2, tiles_per_seq=seq // tm)
    return pl.pallas_call(
        kern,
        grid=(m // tm, nf),
        in_specs=[
            pl.BlockSpec((tm, d), lambda i, f: (i, 0)),
            pl.BlockSpec((1, d), lambda i, f: (0, 0)),
            pl.BlockSpec((None, d, tf), lambda i, f: (layer, 0, f)),
            pl.BlockSpec((None, d, tf), lambda i, f: (layer, 0, f + nf)),
            pl.BlockSpec((FFN_CONV, tf), lambda i, f: (0, f)),
            pl.BlockSpec((1, tf), lambda i, f: (0, f)),
            pl.BlockSpec((None, tf, d), lambda i, f: (layer, f, 0)),
            pl.BlockSpec((1, d), lambda i, f: (0, 0)),
        ],
        out_specs=pl.BlockSpec((tm, d), lambda i, f: (i, 0)),
        out_shape=jax.ShapeDtypeStruct((m, d), F32),
        scratch_shapes=[
            pltpu.VMEM((tm, d), BF16),
            pltpu.VMEM((tm, d), F32),
            pltpu.VMEM((nf, SUBLANES, tf), F32),
        ],
        compiler_params=_params("arbitrary", "arbitrary"),
        name="conv_ffn",
    )(h, gpre.reshape(1, d), w_up, w_up, conv_w, conv_b.reshape(1, dff), w_down, gpost.reshape(1, d))


def _gates_kernel(ba_ref, prm_ref, g_ref, gt_ref):
    c = DELTA_CHUNK
    lane = lax.broadcasted_iota(jnp.int32, (c, LANES), 1)
    is_beta = lane < GDN_HEADS
    is_g = jnp.logical_and(lane >= GDN_HEADS, lane < 2 * GDN_HEADS)
    row = lax.broadcasted_iota(jnp.int32, (c, c), 0)
    col = lax.broadcasted_iota(jnp.int32, (c, c), 1)
    tril = (row >= col).astype(F32)
    for ci in range(gt_ref.shape[0]):
        ba = ba_ref[ci * c:(ci + 1) * c, :]
        g = -jnp.exp(prm_ref[0:1, :]) * jax.nn.softplus(ba + prm_ref[1:2, :])
        g = jnp.where(is_g, g, 0.0)
        gcum = jnp.dot(tril, g, preferred_element_type=F32, precision=lax.Precision.HIGHEST)
        out = jnp.where(is_beta, jax.nn.sigmoid(ba), gcum)
        g_ref[ci * c:(ci + 1) * c, :] = out
        gt_ref[ci] = out.T


def gdn_gates(ba, prm, *, chunks_per_step):
    t = ba.shape[0]
    c = DELTA_CHUNK
    rows = chunks_per_step * c
    return pl.pallas_call(
        _gates_kernel,
        grid=(t // rows,),
        in_specs=[
            pl.BlockSpec((rows, LANES), lambda i: (i, 0)),
            pl.BlockSpec((SUBLANES, LANES), lambda i: (0, 0)),
        ],
        out_specs=[
            pl.BlockSpec((rows, LANES), lambda i: (i, 0)),
            pl.BlockSpec((chunks_per_step, LANES, c), lambda i: (i, 0, 0)),
        ],
        out_shape=[
            jax.ShapeDtypeStruct((t, LANES), F32),
            jax.ShapeDtypeStruct((t // c, LANES, c), F32),
        ],
        compiler_params=_params("parallel"),
        name="gdn_gates",
    )(ba, prm)


def _unit_lower_inverse(lows, masks):
    eye, base_mask, merge_masks = masks
    a_s = [jnp.where(base_mask, -low, 0.0) for low in lows]
    xs = [eye + a for a in a_s]
    span = 2
    while span < INV_BASE:
        a_s = [_bdot(a, a) for a in a_s]
        xs = [x + _bdot(x, a) for x, a in zip(xs, a_s)]
        span *= 2
    for mm in merge_masks:
        ys = [_bdot(x, jnp.where(mm, low, 0.0)) for x, low in zip(xs, lows)]
        xs = [x - _bdot(y, x) for x, y in zip(xs, ys)]
    return xs


def _inverse_masks(c):
    row = lax.broadcasted_iota(jnp.int32, (c, c), 0)
    col = lax.broadcasted_iota(jnp.int32, (c, c), 1)
    eye = (row == col).astype(F32)
    same = lambda b: (row // b) == (col // b)
    base_mask = same(INV_BASE)
    merge_masks = []
    b = INV_BASE
    while b < c:
        merge_masks.append(jnp.logical_and(same(2 * b), jnp.logical_not(same(b))))
        b *= 2
    return eye, base_mask, merge_masks


def _gdn_kernel(q_ref, k_ref, v_ref, z_ref, g_ref, gt_ref, nw_ref, o_ref,
                state_ref, u_ref, wq_ref, kd_ref, ai_ref, eg_ref, *, hb, ts, chunks_a):
    c = DELTA_CHUNK
    nchunk = ts // c
    hg = pl.program_id(1)
    tb = pl.program_id(2)

    @pl.when(tb == 0)
    def _():
        state_ref[...] = jnp.zeros_like(state_ref)

    row = lax.broadcasted_iota(jnp.int32, (c, c), 0)
    col = lax.broadcasted_iota(jnp.int32, (c, c), 1)
    causal = row >= col
    strict = row > col
    inv_masks = _inverse_masks(c)
    lane = lax.broadcasted_iota(jnp.int32, (c, LANES), 1)

    def phase_a(it, carry):
        items = []
        for uu in range(chunks_a):
            ci = it * chunks_a + uu
            r0 = pl.multiple_of(ci * c, c)
            gates = g_ref[pl.ds(r0, c), :]
            for i in range(hb):
                head = hg * hb + i
                lanes = slice(i * LANES, (i + 1) * LANES)
                beta = jnp.sum(jnp.where(lane == head, gates, 0.0), axis=-1, keepdims=True)
                gc = jnp.sum(jnp.where(lane == head + GDN_HEADS, gates, 0.0), axis=-1, keepdims=True)
                gc_row = gt_ref[ci, pl.ds(head + GDN_HEADS, 1), :]
                items.append(dict(ci=ci, r0=r0, i=i, lanes=lanes, beta=beta, gc=gc, gc_row=gc_row,
                                  g_last=gc_row[:, c - 1:c]))
        for d in items:
            d["qn"] = q_ref[pl.ds(d["r0"], c), d["lanes"]]
            d["kn"] = k_ref[pl.ds(d["r0"], c), d["lanes"]]
            d["kb"] = d["kn"].astype(F32) * d["beta"]
        for d in items:
            d["qk"] = lax.dot_general(jnp.concatenate([d["qn"], d["kb"].astype(BF16)], axis=0), d["kn"],
                                      NT_DIMS, preferred_element_type=F32)
        lows = []
        for d in items:
            decay = jnp.exp(jnp.where(causal, d["gc"] - d["gc_row"], -jnp.inf))
            ai_ref[pl.ds(d["r0"], c), d["lanes"]] = jnp.where(causal, d["qk"][:c] * decay, 0.0).astype(BF16)
            lows.append(jnp.where(strict, d["qk"][c:] * decay, 0.0))
        tinvs = _unit_lower_inverse(lows, inv_masks)
        rhss = []
        for d in items:
            d["egc"] = jnp.exp(d["gc"])
            vv = v_ref[pl.ds(d["r0"], c), d["lanes"]].astype(F32)
            rhss.append(jnp.concatenate([vv * d["beta"], d["kb"] * d["egc"]], axis=1))
        sols = [_bdot(tinv, rhs) for tinv, rhs in zip(tinvs, rhss)]
        for d, sol in zip(items, sols):
            r0, lanes = d["r0"], d["lanes"]
            u_ref[pl.ds(r0, c), lanes] = sol[:, :LANES]
            wq_ref[0, pl.ds(r0, c), lanes] = sol[:, LANES:].astype(BF16)
            wq_ref[1, pl.ds(r0, c), lanes] = (d["qn"].astype(F32) * d["egc"]).astype(BF16)
            kd_ref[pl.ds(r0, c), lanes] = (d["kn"].astype(F32) * jnp.exp(d["g_last"] - d["gc"])).astype(BF16)
            eg_ref[d["ci"], d["i"]] = jnp.broadcast_to(jnp.exp(d["g_last"]), (SUBLANES, LANES))
        return carry

    lax.fori_loop(0, nchunk // chunks_a, phase_a, 0)

    def phase_b(ci, carry):
        r0 = pl.multiple_of(ci * c, c)
        heads = range(hb)
        lanes = [slice(i * LANES, (i + 1) * LANES) for i in heads]
        states = [state_ref[i] for i in heads]
        wss = [jnp.dot(jnp.concatenate([wq_ref[0, pl.ds(r0, c), lanes[i]], wq_ref[1, pl.ds(r0, c), lanes[i]]],
                                       axis=0),
                       states[i].astype(BF16), preferred_element_type=F32) for i in heads]
        v_news = [(u_ref[pl.ds(r0, c), lanes[i]] - wss[i][:c]).astype(BF16) for i in heads]
        upds = [lax.dot_general(kd_ref[pl.ds(r0, c), lanes[i]], v_news[i], TN_DIMS, preferred_element_type=F32)
                for i in heads]
        for i in heads:
            state_ref[i] = states[i] * eg_ref[ci, i][0:1, :] + upds[i]
        outs = [wss[i][c:] + jnp.dot(ai_ref[pl.ds(r0, c), lanes[i]], v_news[i], preferred_element_type=F32)
                for i in heads]
        for i in heads:
            z = z_ref[pl.ds(r0, c), lanes[i]].astype(F32)
            o_ref[pl.ds(r0, c), lanes[i]] = (_rms_scale(outs[i], nw_ref[...]) * _silu(z)).astype(o_ref.dtype)
        return carry

    lax.fori_loop(0, nchunk, phase_b, 0)


def gdn_core(proj, gates, gates_t, norm_w, *, batch, seq, hb, ts, chunks_a):
    t = proj.shape[0]
    w = hb * LANES
    ngroups = GDN_HEADS // hb
    nt = seq // ts
    nchunk = ts // DELTA_CHUNK
    kern = functools.partial(_gdn_kernel, hb=hb, ts=ts, chunks_a=chunks_a)

    def col_spec(base):
        return pl.BlockSpec((ts, w), lambda b, h, s: (b * nt + s, base * ngroups + h))

    return pl.pallas_call(
        kern,
        grid=(batch, ngroups, nt),
        in_specs=[
            col_spec(0), col_spec(1), col_spec(2), col_spec(3),
            pl.BlockSpec((ts, LANES), lambda b, h, s: (b * nt + s, 0)),
            pl.BlockSpec((nchunk, LANES, DELTA_CHUNK), lambda b, h, s: (b * nt + s, 0, 0)),
            pl.BlockSpec((1, LANES), lambda b, h, s: (0, 0)),
        ],
        out_specs=pl.BlockSpec((ts, w), lambda b, h, s: (b * nt + s, h)),
        out_shape=jax.ShapeDtypeStruct((t, GDN_V_DIM), BF16),
        scratch_shapes=[
            pltpu.VMEM((hb, GDN_DK, GDN_DV), F32),
            pltpu.VMEM((ts, w), F32),
            pltpu.VMEM((2, ts, w), BF16),
            pltpu.VMEM((ts, w), BF16),
            pltpu.VMEM((ts, w), BF16),
            pltpu.VMEM((nchunk, hb, SUBLANES, LANES), F32),
        ],
        compiler_params=_params("parallel", "parallel", "arbitrary"),
        name="gdn_core",
    )(proj, proj, proj, proj, gates, gates_t, norm_w.reshape(1, LANES))


def _attn_kernel(qi_tab, ki_tab, q_ref, k_ref, v_ref, lam_ref, sw_ref, o_ref,
                 m_ref, l_ref, acc_ref, *, tq, tk, lambda_init):
    p = pl.program_id(2)
    qi = qi_tab[p]
    ki = ki_tab[p]
    dh = DIFF_DH

    @pl.when(ki == 0)
    def _():
        m_ref[...] = jnp.full_like(m_ref, -jnp.inf)
        l_ref[...] = jnp.zeros_like(l_ref)
        acc_ref[...] = jnp.zeros_like(acc_ref)

    parts = tq // tk
    row = lax.broadcasted_iota(jnp.int32, (tk, tk), 0)
    col = lax.broadcasted_iota(jnp.int32, (tk, tk), 1)
    diag_visible = (col // CHUNK) <= (row // CHUNK)

    def update(diag_part):
        k = k_ref[...]
        v = v_ref[...]
        first = 0 if diag_part is None else diag_part
        chains = [(r, m) for r in range(first, parts) for m in range(2)]

        def scores(r, m):
            return lax.dot_general(q_ref[r * tk:(r + 1) * tk, m * dh:(m + 1) * dh], k[:, m * dh:(m + 1) * dh],
                                   NT_DIMS, preferred_element_type=F32)

        s_next = scores(*chains[0])
        for idx, (r, m) in enumerate(chains):
            s = s_next
            if idx + 1 < len(chains):
                s_next = scores(*chains[idx + 1])
            rows = slice(r * tk, (r + 1) * tk)
            if r == diag_part:
                s = jnp.where(diag_visible, s, -jnp.inf)
            m_old = m_ref[m, rows, :]
            m_new = jnp.maximum(m_old, jnp.max(s, axis=-1, keepdims=True))
            alpha = jnp.exp2(m_old - m_new)
            pr = jnp.exp2(s - jnp.concatenate([m_new] * (tk // LANES), axis=1))
            psum = pr[:, 0:LANES]
            for t in range(1, tk // LANES):
                psum = psum + pr[:, t * LANES:(t + 1) * LANES]
            l_ref[m, rows, :] = alpha * l_ref[m, rows, :] + psum
            acc_ref[m, rows, :] = (jnp.concatenate([alpha] * (2 * dh // LANES), axis=1) * acc_ref[m, rows, :]
                                   + jnp.dot(pr.astype(BF16), v, preferred_element_type=F32))
            m_ref[m, rows, :] = m_new

    d = ki - qi * parts
    pl.when(d < 0)(functools.partial(update, None))
    for j in range(parts):
        pl.when(d == j)(functools.partial(update, j))

    @pl.when(d == parts - 1)
    def _():
        lam = (jnp.exp(jnp.sum(lam_ref[0:1, :] * lam_ref[1:2, :], axis=-1, keepdims=True))
               - jnp.exp(jnp.sum(lam_ref[2:3, :] * lam_ref[3:4, :], axis=-1, keepdims=True))
               + lambda_init)
        inv_l0 = 1.0 / jnp.sum(l_ref[0], axis=-1, keepdims=True)
        inv_l1 = lam / jnp.sum(l_ref[1], axis=-1, keepdims=True)
        a = acc_ref[0] * inv_l0 - acc_ref[1] * inv_l1
        o_ref[...] = (_rms_scale(a, sw_ref[...]) * (1.0 - lambda_init)).astype(o_ref.dtype)


def diff_attention(q, kv, lam_rows, subln_w, *, batch, seq, tq, tk, lambda_init, q_col0):
    t = q.shape[0]
    hw = 2 * DIFF_DH
    nq = seq // tq
    nk = seq // tk
    pairs = [(a, b) for a in range(nq) for b in range(nk) if b * tk < (a + 1) * tq]
    qi_tab = jnp.asarray([a for a, _ in pairs], jnp.int32)
    ki_tab = jnp.asarray([b for _, b in pairs], jnp.int32)
    kern = functools.partial(_attn_kernel, tq=tq, tk=tk, lambda_init=lambda_init)
    grid_spec = pltpu.PrefetchScalarGridSpec(
        num_scalar_prefetch=2,
        grid=(batch, DIFF_HEADS, len(pairs)),
        in_specs=[
            pl.BlockSpec((tq, hw), lambda b, h, p, qt, kt: (b * nq + qt[p], q_col0 + h)),
            pl.BlockSpec((tk, hw), lambda b, h, p, qt, kt: (b * nk + kt[p], h)),
            pl.BlockSpec((tk, hw), lambda b, h, p, qt, kt: (b * nk + kt[p], DIFF_HEADS + h)),
            pl.BlockSpec((SUBLANES, DIFF_DH), lambda b, h, p, qt, kt: (0, 0)),
            pl.BlockSpec((1, hw), lambda b, h, p, qt, kt: (0, 0)),
        ],
        out_specs=pl.BlockSpec((tq, hw), lambda b, h, p, qt, kt: (b * nq + qt[p], h)),
        scratch_shapes=[
            pltpu.VMEM((2, tq, LANES), F32),
            pltpu.VMEM((2, tq, LANES), F32),
            pltpu.VMEM((2, tq, hw), F32),
        ],
    )
    return pl.pallas_call(
        kern,
        grid_spec=grid_spec,
        out_shape=jax.ShapeDtypeStruct((t, DIFF_HEADS * hw), BF16),
        compiler_params=_params("parallel", "parallel", "arbitrary"),
        name="diff_attention",
    )(qi_tab, ki_tab, q, kv, kv, lam_rows, subln_w.reshape(1, hw))


def kernel(x, norm_mix_pre, norm_mix_post, norm_ffn_pre, norm_ffn_post, gdn_w_in, gdn_conv_w, gdn_a_log,
           gdn_dt_bias, gdn_norm_w, gdn_w_out, kv_norm_w, w_kv, diff_w_q, diff_lq1, diff_lk1, diff_lq2,
           diff_lk2, diff_subln_w, diff_w_o, ffn_w_up, ffn_conv_w, ffn_conv_b, ffn_w_down):
    batch, seq, d = x.shape
    t = batch * seq
    h = x.reshape(t, d)
    n_main = 2 * GDN_QK_DIM + 2 * GDN_V_DIM
    ffn_w_up_b = ffn_w_up.astype(BF16)
    ffn_w_down_b = ffn_w_down.astype(BF16)

    for layer in range(DEPTH):
        if layer < N_A:
            w_in = gdn_w_in[layer].astype(BF16)
            w_ba = jnp.pad(w_in[:, n_main:], ((0, 0), (0, LANES - 2 * GDN_HEADS)))
            proj, ba = gdn_in_proj(h, norm_mix_pre[layer], w_in, w_ba, gdn_conv_w[layer], seq=seq, tm=1024,
                                   tn=512, row_parts=4)
            prm = jnp.zeros((SUBLANES, LANES), F32)
            prm = prm.at[0, GDN_HEADS:2 * GDN_HEADS].set(gdn_a_log[layer].astype(F32))
            prm = prm.at[1, GDN_HEADS:2 * GDN_HEADS].set(gdn_dt_bias[layer].astype(F32))
            gates, gates_t = gdn_gates(ba, prm, chunks_per_step=4)
            o = gdn_core(proj, gates, gates_t, gdn_norm_w[layer], batch=batch, seq=seq, hb=8, ts=1024,
                         chunks_a=2)
            h = matmul_norm_res(o, gdn_w_out[layer].astype(BF16), h, norm_mix_post[layer], tm=512)
        else:
            j = layer - N_A
            lambda_init = 0.8 - 0.6 * math.exp(-0.3 * layer)
            q_scale = DIFF_DH ** -0.5 * math.log2(math.e)
            tn = 512
            if layer == N_A:
                w_kvq = jnp.concatenate([w_kv, diff_w_q[j]], axis=1).astype(BF16)
                kv = norm_matmul(h, jnp.stack([kv_norm_w, norm_mix_pre[layer]]), w_kvq, BF16, tm=1024, tn=tn,
                                 split=w_kv.shape[1] // tn, out_scale=q_scale)
                q, q_col0 = kv, w_kv.shape[1] // (2 * DIFF_DH)
            else:
                q = norm_matmul(h, norm_mix_pre[layer][None], diff_w_q[j].astype(BF16), BF16, tm=1024, tn=tn,
                                out_scale=q_scale)
                q_col0 = 0
            lam_rows = jnp.zeros((SUBLANES, DIFF_DH), F32)
            lam_rows = lam_rows.at[0].set(diff_lq1[j]).at[1].set(diff_lk1[j])
            lam_rows = lam_rows.at[2].set(diff_lq2[j]).at[3].set(diff_lk2[j])
            att = diff_attention(q, kv, lam_rows, diff_subln_w[j], batch=batch, seq=seq, tq=2048, tk=512,
                                 lambda_init=lambda_init, q_col0=q_col0)
            h = matmul_norm_res(att, diff_w_o[j].astype(BF16), h, norm_mix_post[layer], tm=512)
        h = conv_ffn(h, norm_ffn_pre[layer], ffn_w_up_b, ffn_conv_w[layer], ffn_conv_b[layer], ffn_w_down_b,
                     norm_ffn_post[layer], layer=layer, seq=seq, tm=512, tf=512)
    return h.reshape(batch, seq, d)
```

```python
import functools
import math

import jax
import jax.numpy as jnp
from jax import lax
from jax.experimental import pallas as pl
from jax.experimental.pallas import tpu as pltpu

D_MODEL = 2048
DEPTH = 2
N_A = DEPTH // 2
CHUNK = 64
GDN_HEADS = 16
GDN_DK = 128
GDN_DV = 128
GDN_CONV = 4
GDN_QK_DIM = GDN_HEADS * GDN_DK
GDN_V_DIM = GDN_HEADS * GDN_DV
DIFF_HEADS = 8
DIFF_DH = 128
DIFF_QK_DIM = DIFF_HEADS * 2 * DIFF_DH
D_FF = 5632
FFN_CONV = 3
EPS = 1e-6

LANES = 128
SUBLANES = 8
VMEM_LIMIT_BYTES = 56 * 1024 * 1024

DELTA_CHUNK = 128
INV_BASE = 16

F32 = jnp.float32
BF16 = jnp.bfloat16
NT_DIMS = (((1,), (1,)), ((), ()))
TN_DIMS = (((0,), (0,)), ((), ()))


def _bdot(a, b):
    return jnp.dot(a.astype(BF16), b.astype(BF16), preferred_element_type=F32)


def _rms_scale(x, gain):
    ms = jnp.mean(x * x, axis=-1, keepdims=True)
    return x * lax.rsqrt(ms + EPS) * gain


def _silu(x):
    return x * jax.nn.sigmoid(x)


def _params(*sem):
    return pltpu.CompilerParams(dimension_semantics=sem, vmem_limit_bytes=VMEM_LIMIT_BYTES)


def _causal_delay(y, prev, shift):
    head = jnp.concatenate([prev, y[0:SUBLANES, :]], axis=0)
    first = pltpu.roll(head, shift, axis=0)[SUBLANES:, :]
    body = pltpu.roll(y, shift, axis=0)
    return jnp.concatenate([first, body[SUBLANES:, :]], axis=0)


def _norm_matmul_kernel(x_ref, g_ref, w_ref, o_ref, xn_ref, *, split, out_scale):
    j = pl.program_id(1)
    nseg = g_ref.shape[0]

    @pl.when(j == 0)
    def _():
        x = x_ref[...]
        xh = x * lax.rsqrt(jnp.mean(x * x, axis=-1, keepdims=True) + EPS)
        for s in range(nseg):
            xn_ref[s] = (xh * g_ref[s:s + 1, :]).astype(BF16)

    if nseg == 1:
        y = jnp.dot(xn_ref[0], w_ref[...], preferred_element_type=F32)
        if out_scale is not None:
            y = y * out_scale
    else:
        second = j >= split
        y = jnp.dot(xn_ref[second.astype(jnp.int32)], w_ref[...], preferred_element_type=F32)
        if out_scale is not None:
            y = y * jnp.where(second, out_scale, 1.0)
    o_ref[...] = y.astype(o_ref.dtype)


def norm_matmul(x, gains, w, out_dtype, *, tm, tn, split=None, out_scale=None):
    m, k = x.shape
    n = w.shape[1]
    nseg = gains.shape[0]
    return pl.pallas_call(
        functools.partial(_norm_matmul_kernel, split=split, out_scale=out_scale),
        grid=(m // tm, n // tn),
        in_specs=[
            pl.BlockSpec((tm, k), lambda i, j: (i, 0)),
            pl.BlockSpec((nseg, k), lambda i, j: (0, 0)),
            pl.BlockSpec((k, tn), lambda i, j: (0, j)),
        ],
        out_specs=pl.BlockSpec((tm, tn), lambda i, j: (i, j)),
        out_shape=jax.ShapeDtypeStruct((m, n), out_dtype),
        scratch_shapes=[pltpu.VMEM((nseg, tm, k), BF16)],
        compiler_params=_params("parallel", "arbitrary"),
        name="norm_matmul",
    )(x, gains, w)


def _gdn_in_proj_kernel(x_ref, g_ref, w_ref, wba_ref, cw_ref, o_ref, ba_ref, xn_ref, carry_ref, *,
                        tm, row_parts, tiles_per_seq, n_q_tiles, n_norm_tiles, n_conv_tiles):
    i = pl.program_id(0)
    j = pl.program_id(1)
    tn = w_ref.shape[1]

    @pl.when(j == 0)
    def _():
        xn_ref[...] = _rms_scale(x_ref[...], g_ref[...]).astype(BF16)
        ba_ref[...] = jnp.dot(xn_ref[...], wba_ref[...], preferred_element_type=F32)

    rp = tm // row_parts

    def project():
        return [jnp.dot(xn_ref[r * rp:(r + 1) * rp, :], w_ref[...], preferred_element_type=F32)
                for r in range(row_parts)]

    def conv_silu(y, prev):
        acc = None
        for tap in range(GDN_CONV):
            delay = GDN_CONV - 1 - tap
            yd = y if delay == 0 else _causal_delay(y, prev, delay)
            term = yd * cw_ref[tap:tap + 1, :]
            acc = term if acc is None else acc + term
        return _silu(acc)

    def epilogue(normalize):
        ys = project()
        seq_start = (i % tiles_per_seq) == 0
        prev = jnp.where(seq_start, 0.0, carry_ref[j])
        carry_ref[j] = ys[-1][rp - SUBLANES:, :]
        scale = jnp.where(j < n_q_tiles, GDN_DK ** -0.5, 1.0)
        for r, y in enumerate(ys):
            act = conv_silu(y, prev)
            prev = y[rp - SUBLANES:, :]
            if normalize:
                heads = []
                for c0 in range(0, tn, GDN_DK):
                    a = act[:, c0:c0 + GDN_DK]
                    heads.append(a * (lax.rsqrt(jnp.sum(a * a, axis=-1, keepdims=True) + EPS) * scale))
                act = jnp.concatenate(heads, axis=1)
            o_ref[r * rp:(r + 1) * rp, :] = act.astype(o_ref.dtype)

    pl.when(j < n_norm_tiles)(functools.partial(epilogue, True))
    pl.when(jnp.logical_and(j >= n_norm_tiles, j < n_conv_tiles))(functools.partial(epilogue, False))

    @pl.when(j >= n_conv_tiles)
    def _():
        for r, y in enumerate(project()):
            o_ref[r * rp:(r + 1) * rp, :] = y.astype(o_ref.dtype)


def gdn_in_proj(x, gain, w_in, w_ba, conv_w, *, seq, tm, tn, row_parts):
    m, k = x.shape
    n_main = 2 * GDN_QK_DIM + 2 * GDN_V_DIM
    n_conv_tiles = (2 * GDN_QK_DIM + GDN_V_DIM) // tn
    kern = functools.partial(
        _gdn_in_proj_kernel, tm=tm, row_parts=row_parts, tiles_per_seq=seq // tm, n_q_tiles=GDN_QK_DIM // tn,
        n_norm_tiles=2 * GDN_QK_DIM // tn, n_conv_tiles=n_conv_tiles)
    return pl.pallas_call(
        kern,
        grid=(m // tm, n_main // tn),
        in_specs=[
            pl.BlockSpec((tm, k), lambda i, j: (i, 0)),
            pl.BlockSpec((1, k), lambda i, j: (0, 0)),
            pl.BlockSpec((k, tn), lambda i, j: (0, j)),
            pl.BlockSpec((k, LANES), lambda i, j: (0, 0)),
            pl.BlockSpec((GDN_CONV, tn), lambda i, j: (0, jnp.minimum(j, n_conv_tiles - 1))),
        ],
        out_specs=[
            pl.BlockSpec((tm, tn), lambda i, j: (i, j)),
            pl.BlockSpec((tm, LANES), lambda i, j: (i, 0)),
        ],
        out_shape=[
            jax.ShapeDtypeStruct((m, n_main), BF16),
            jax.ShapeDtypeStruct((m, LANES), F32),
        ],
        scratch_shapes=[
            pltpu.VMEM((tm, k), BF16),
            pltpu.VMEM((n_conv_tiles, SUBLANES, tn), F32),
        ],
        compiler_params=_params("arbitrary", "arbitrary"),
        name="gdn_in_proj",
    )(x, gain.reshape(1, k), w_in, w_ba, conv_w)


def _matmul_norm_res_kernel(a_ref, w_ref, h_ref, g_ref, o_ref):
    m = jnp.dot(a_ref[...], w_ref[...], preferred_element_type=F32)
    o_ref[...] = h_ref[...] + _rms_scale(m, g_ref[...])


def matmul_norm_res(a, w, h, gain, *, tm):
    m, k = a.shape
    n = w.shape[1]
    return pl.pallas_call(
        _matmul_norm_res_kernel,
        grid=(m // tm,),
        in_specs=[
            pl.BlockSpec((tm, k), lambda i: (i, 0)),
            pl.BlockSpec((k, n), lambda i: (0, 0)),
            pl.BlockSpec((tm, n), lambda i: (i, 0)),
            pl.BlockSpec((1, n), lambda i: (0, 0)),
        ],
        out_specs=pl.BlockSpec((tm, n), lambda i: (i, 0)),
        out_shape=jax.ShapeDtypeStruct((m, n), F32),
        compiler_params=_params("parallel"),
        name="matmul_norm_res",
    )(a, w, h, gain.reshape(1, n))


def _ffn_kernel(h_ref, gpre_ref, wg_ref, wu_ref, cw_ref, cb_ref, wd_ref, gpost_ref, o_ref,
                xn_ref, acc_ref, carry_ref, *, tm, tiles_per_seq):
    i = pl.program_id(0)
    f = pl.program_id(1)

    @pl.when(f == 0)
    def _():
        xn_ref[...] = _rms_scale(h_ref[...], gpre_ref[...]).astype(BF16)
        acc_ref[...] = jnp.zeros_like(acc_ref)

    xn = xn_ref[...]
    gate = jnp.dot(xn, wg_ref[...], preferred_element_type=F32)
    up = jnp.dot(xn, wu_ref[...], preferred_element_type=F32)

    seq_start = (i % tiles_per_seq) == 0
    prev = jnp.where(seq_start, 0.0, carry_ref[f])
    carry_ref[f] = gate[tm - SUBLANES:, :]
    conv = (_causal_delay(gate, prev, 2) * cw_ref[0:1, :] + _causal_delay(gate, prev, 1) * cw_ref[1:2, :]
            + gate * cw_ref[2:3, :] + cb_ref[...])
    act = (_silu(conv) * up).astype(BF16)
    acc_ref[...] += jnp.dot(act, wd_ref[...], preferred_element_type=F32)

    @pl.when(f == pl.num_programs(1) - 1)
    def _():
        o_ref[...] = h_ref[...] + _rms_scale(acc_ref[...], gpost_ref[...])


def conv_ffn(h, gpre, w_up, conv_w, conv_b, w_down, gpost, *, seq, tm, tf):
    m, d = h.shape
    dff = w_down.shape[0]
    nf = dff // tf
    kern = functools.partial(_ffn_kernel, tm=tm, tiles_per_seq=seq // tm)
    return pl.pallas_call(
        kern,
        grid=(m // tm, nf),
        in_specs=[
            pl.BlockSpec((tm, d), lambda i, f: (i, 0)),
            pl.BlockSpec((1, d), lambda i, f: (0, 0)),
            pl.BlockSpec((d, tf), lambda i, f: (0, f)),
            pl.BlockSpec((d, tf), lambda i, f: (0, f + nf)),
            pl.BlockSpec((FFN_CONV, tf), lambda i, f: (0, f)),
            pl.BlockSpec((1, tf), lambda i, f: (0, f)),
            pl.BlockSpec((tf, d), lambda i, f: (f, 0)),
            pl.BlockSpec((1, d), lambda i, f: (0, 0)),
        ],
        out_specs=pl.BlockSpec((tm, d), lambda i, f: (i, 0)),
        out_shape=jax.ShapeDtypeStruct((m, d), F32),
        scratch_shapes=[
            pltpu.VMEM((tm, d), BF16),
            pltpu.VMEM((tm, d), F32),
            pltpu.VMEM((nf, SUBLANES, tf), F32),
        ],
        compiler_params=_params("arbitrary", "arbitrary"),
        name="conv_ffn",
    )(h, gpre.reshape(1, d), w_up, w_up, conv_w, conv_b.reshape(1, dff), w_down, gpost.reshape(1, d))


def _gates_kernel(ba_ref, prm_ref, g_ref, gt_ref):
    c = DELTA_CHUNK
    lane = lax.broadcasted_iota(jnp.int32, (c, LANES), 1)
    is_beta = lane < GDN_HEADS
    is_g = jnp.logical_and(lane >= GDN_HEADS, lane < 2 * GDN_HEADS)
    row = lax.broadcasted_iota(jnp.int32, (c, c), 0)
    col = lax.broadcasted_iota(jnp.int32, (c, c), 1)
    tril = (row >= col).astype(F32)
    for ci in range(gt_ref.shape[0]):
        ba = ba_ref[ci * c:(ci + 1) * c, :]
        g = -jnp.exp(prm_ref[0:1, :]) * jax.nn.softplus(ba + prm_ref[1:2, :])
        g = jnp.where(is_g, g, 0.0)
        gcum = jnp.dot(tril, g, preferred_element_type=F32, precision=lax.Precision.HIGHEST)
        out = jnp.where(is_beta, jax.nn.sigmoid(ba), gcum)
        g_ref[ci * c:(ci + 1) * c, :] = out
        gt_ref[ci] = out.T


def gdn_gates(ba, prm, *, chunks_per_step):
    t = ba.shape[0]
    c = DELTA_CHUNK
    rows = chunks_per_step * c
    return pl.pallas_call(
        _gates_kernel,
        grid=(t // rows,),
        in_specs=[
            pl.BlockSpec((rows, LANES), lambda i: (i, 0)),
            pl.BlockSpec((SUBLANES, LANES), lambda i: (0, 0)),
        ],
        out_specs=[
            pl.BlockSpec((rows, LANES), lambda i: (i, 0)),
            pl.BlockSpec((chunks_per_step, LANES, c), lambda i: (i, 0, 0)),
        ],
        out_shape=[
            jax.ShapeDtypeStruct((t, LANES), F32),
            jax.ShapeDtypeStruct((t // c, LANES, c), F32),
        ],
        compiler_params=_params("parallel"),
        name="gdn_gates",
    )(ba, prm)


def _unit_lower_inverse(lows, masks):
    eye, base_mask, merge_masks = masks
    a_s = [jnp.where(base_mask, -low, 0.0) for low in lows]
    xs = [eye + a for a in a_s]
    span = 2
    while span < INV_BASE:
        a_s = [_bdot(a, a) for a in a_s]
        xs = [x + _bdot(x, a) for x, a in zip(xs, a_s)]
        span *= 2
    for mm in merge_masks:
        ys = [_bdot(x, jnp.where(mm, low, 0.0)) for x, low in zip(xs, lows)]
        xs = [x - _bdot(y, x) for x, y in zip(xs, ys)]
    return xs


def _inverse_masks(c):
    row = lax.broadcasted_iota(jnp.int32, (c, c), 0)
    col = lax.broadcasted_iota(jnp.int32, (c, c), 1)
    eye = (row == col).astype(F32)
    same = lambda b: (row // b) == (col // b)
    base_mask = same(INV_BASE)
    merge_masks = []
    b = INV_BASE
    while b < c:
        merge_masks.append(jnp.logical_and(same(2 * b), jnp.logical_not(same(b))))
        b *= 2
    return eye, base_mask, merge_masks


def _gdn_kernel(q_ref, k_ref, v_ref, z_ref, g_ref, gt_ref, nw_ref, wup_ref, wdn_ref, o_ref, wup_b_ref, wdn_b_ref,
                state_ref, u_ref, wq_ref, kd_ref, ai_ref, eg_ref, *, hb, ts, chunks_a):
    c = DELTA_CHUNK
    nchunk = ts // c
    hg = pl.program_id(1)
    tb = pl.program_id(2)

    wup_b_ref[...] = wup_ref[...].astype(BF16)
    wdn_b_ref[...] = wdn_ref[...].astype(BF16)

    @pl.when(tb == 0)
    def _():
        state_ref[...] = jnp.zeros_like(state_ref)

    row = lax.broadcasted_iota(jnp.int32, (c, c), 0)
    col = lax.broadcasted_iota(jnp.int32, (c, c), 1)
    causal = row >= col
    strict = row > col
    inv_masks = _inverse_masks(c)
    lane = lax.broadcasted_iota(jnp.int32, (c, LANES), 1)

    def phase_a(it, carry):
        items = []
        for uu in range(chunks_a):
            ci = it * chunks_a + uu
            r0 = pl.multiple_of(ci * c, c)
            gates = g_ref[pl.ds(r0, c), :]
            for i in range(hb):
                head = hg * hb + i
                lanes = slice(i * LANES, (i + 1) * LANES)
                beta = jnp.sum(jnp.where(lane == head, gates, 0.0), axis=-1, keepdims=True)
                gc = jnp.sum(jnp.where(lane == head + GDN_HEADS, gates, 0.0), axis=-1, keepdims=True)
                gc_row = gt_ref[ci, pl.ds(head + GDN_HEADS, 1), :]
                items.append(dict(ci=ci, r0=r0, i=i, lanes=lanes, beta=beta, gc=gc, gc_row=gc_row,
                                  g_last=gc_row[:, c - 1:c]))
        for d in items:
            d["qn"] = q_ref[pl.ds(d["r0"], c), d["lanes"]]
            d["kn"] = k_ref[pl.ds(d["r0"], c), d["lanes"]]
            d["kb"] = d["kn"].astype(F32) * d["beta"]
        for d in items:
            d["qk"] = lax.dot_general(jnp.concatenate([d["qn"], d["kb"].astype(BF16)], axis=0), d["kn"],
                                      NT_DIMS, preferred_element_type=F32)
        lows = []
        for d in items:
            decay = jnp.exp(jnp.where(causal, d["gc"] - d["gc_row"], -jnp.inf))
            ai_ref[pl.ds(d["r0"], c), d["lanes"]] = jnp.where(causal, d["qk"][:c] * decay, 0.0).astype(BF16)
            lows.append(jnp.where(strict, d["qk"][c:] * decay, 0.0))
        tinvs = _unit_lower_inverse(lows, inv_masks)
        rhss = []
        for d in items:
            d["egc"] = jnp.exp(d["gc"])
            vv = v_ref[pl.ds(d["r0"], c), d["lanes"]].astype(F32)
            rhss.append(jnp.concatenate([vv * d["beta"], d["kb"] * d["egc"]], axis=1))
        sols = [_bdot(tinv, rhs) for tinv, rhs in zip(tinvs, rhss)]
        for d, sol in zip(items, sols):
            r0, lanes = d["r0"], d["lanes"]
            u_ref[pl.ds(r0, c), lanes] = sol[:, :LANES]
            wq_ref[0, pl.ds(r0, c), lanes] = sol[:, LANES:].astype(BF16)
            wq_ref[1, pl.ds(r0, c), lanes] = (d["qn"].astype(F32) * d["egc"]).astype(BF16)
            kd_ref[pl.ds(r0, c), lanes] = (d["kn"].astype(F32) * jnp.exp(d["g_last"] - d["gc"])).astype(BF16)
            eg_ref[d["ci"], d["i"]] = jnp.broadcast_to(jnp.exp(d["g_last"]), (SUBLANES, LANES))
        return carry

    lax.fori_loop(0, nchunk // chunks_a, phase_a, 0)

    def phase_b(ci, carry):
        r0 = pl.multiple_of(ci * c, c)
        heads = range(hb)
        lanes = [slice(i * LANES, (i + 1) * LANES) for i in heads]
        states = [state_ref[i] for i in heads]
        wss = [jnp.dot(jnp.concatenate([wq_ref[0, pl.ds(r0, c), lanes[i]], wq_ref[1, pl.ds(r0, c), lanes[i]]],
                                       axis=0),
                       states[i].astype(BF16), preferred_element_type=F32) for i in heads]
        v_news = [(u_ref[pl.ds(r0, c), lanes[i]] - wss[i][:c]).astype(BF16) for i in heads]
        upds = [lax.dot_general(kd_ref[pl.ds(r0, c), lanes[i]], v_news[i], TN_DIMS, preferred_element_type=F32)
                for i in heads]
        for i in heads:
            state_ref[i] = states[i] * eg_ref[ci, i][0:1, :] + upds[i]
        outs = [wss[i][c:] + jnp.dot(ai_ref[pl.ds(r0, c), lanes[i]], v_news[i], preferred_element_type=F32)
                for i in heads]
        for i in heads:
            z = z_ref[pl.ds(r0, c), lanes[i]].astype(F32)
            o_ref[pl.ds(r0, c), lanes[i]] = (_rms_scale(outs[i], nw_ref[...]) * _silu(z)).astype(o_ref.dtype)
        return carry

    lax.fori_loop(0, nchunk, phase_b, 0)


def gdn_core(proj, gates, gates_t, norm_w, ffn_w_up, ffn_w_down, *, layer, batch, seq, hb, ts, chunks_a):
    t = proj.shape[0]
    w = hb * LANES
    ngroups = GDN_HEADS // hb
    nt = seq // ts
    nchunk = ts // DELTA_CHUNK
    nsteps = batch * ngroups * nt
    _, up_rows, up_cols = ffn_w_up.shape
    _, dn_rows, dn_cols = ffn_w_down.shape
    kern = functools.partial(_gdn_kernel, hb=hb, ts=ts, chunks_a=chunks_a)

    def col_spec(base):
        return pl.BlockSpec((ts, w), lambda b, h, s: (b * nt + s, base * ngroups + h))

    def step(b, h, s):
        return (b * ngroups + h) * nt + s

    return pl.pallas_call(
        kern,
        grid=(batch, ngroups, nt),
        in_specs=[
            col_spec(0), col_spec(1), col_spec(2), col_spec(3),
            pl.BlockSpec((ts, LANES), lambda b, h, s: (b * nt + s, 0)),
            pl.BlockSpec((nchunk, LANES, DELTA_CHUNK), lambda b, h, s: (b * nt + s, 0, 0)),
            pl.BlockSpec((1, LANES), lambda b, h, s: (0, 0)),
            pl.BlockSpec((None, up_rows // nsteps, up_cols), lambda b, h, s: (layer, step(b, h, s), 0)),
            pl.BlockSpec((None, dn_rows // nsteps, dn_cols), lambda b, h, s: (layer, step(b, h, s), 0)),
        ],
        out_specs=[
            pl.BlockSpec((ts, w), lambda b, h, s: (b * nt + s, h)),
            pl.BlockSpec((up_rows // nsteps, up_cols), lambda b, h, s: (step(b, h, s), 0)),
            pl.BlockSpec((dn_rows // nsteps, dn_cols), lambda b, h, s: (step(b, h, s), 0)),
        ],
        out_shape=[
            jax.ShapeDtypeStruct((t, GDN_V_DIM), BF16),
            jax.ShapeDtypeStruct((up_rows, up_cols), BF16),
            jax.ShapeDtypeStruct((dn_rows, dn_cols), BF16),
        ],
        scratch_shapes=[
            pltpu.VMEM((hb, GDN_DK, GDN_DV), F32),
            pltpu.VMEM((ts, w), F32),
            pltpu.VMEM((2, ts, w), BF16),
            pltpu.VMEM((ts, w), BF16),
            pltpu.VMEM((ts, w), BF16),
            pltpu.VMEM((nchunk, hb, SUBLANES, LANES), F32),
        ],
        compiler_params=_params("parallel", "parallel", "arbitrary"),
        name="gdn_core",
    )(proj, proj, proj, proj, gates, gates_t, norm_w.reshape(1, LANES), ffn_w_up, ffn_w_down)


def _attn_kernel(qi_tab, ki_tab, q_ref, k_ref, v_ref, lam_ref, sw_ref, wup_ref, wdn_ref, o_ref, wup_b_ref, wdn_b_ref,
                 m_ref, l_ref, acc_ref, *, tq, tk, lambda_init):
    p = pl.program_id(2)
    qi = qi_tab[p]
    ki = ki_tab[p]
    dh = DIFF_DH

    @pl.when(p == 0)
    def _():
        wup_b_ref[...] = wup_ref[...].astype(BF16)
        wdn_b_ref[...] = wdn_ref[...].astype(BF16)

    @pl.when(ki == 0)
    def _():
        m_ref[...] = jnp.full_like(m_ref, -jnp.inf)
        l_ref[...] = jnp.zeros_like(l_ref)
        acc_ref[...] = jnp.zeros_like(acc_ref)

    parts = tq // tk
    row = lax.broadcasted_iota(jnp.int32, (tk, tk), 0)
    col = lax.broadcasted_iota(jnp.int32, (tk, tk), 1)
    diag_visible = (col // CHUNK) <= (row // CHUNK)

    def update(diag_part):
        k = k_ref[...]
        v = v_ref[...]
        first = 0 if diag_part is None else diag_part
        chains = [(r, m) for r in range(first, parts) for m in range(2)]

        def scores(r, m):
            return lax.dot_general(q_ref[r * tk:(r + 1) * tk, m * dh:(m + 1) * dh], k[:, m * dh:(m + 1) * dh],
                                   NT_DIMS, preferred_element_type=F32)

        s_next = scores(*chains[0])
        for idx, (r, m) in enumerate(chains):
            s = s_next
            if idx + 1 < len(chains):
                s_next = scores(*chains[idx + 1])
            rows = slice(r * tk, (r + 1) * tk)
            if r == diag_part:
                s = jnp.where(diag_visible, s, -jnp.inf)
            m_old = m_ref[m, rows, :]
            m_new = jnp.maximum(m_old, jnp.max(s, axis=-1, keepdims=True))
            alpha = jnp.exp2(m_old - m_new)
            pr = jnp.exp2(s - jnp.concatenate([m_new] * (tk // LANES), axis=1))
            psum = pr[:, 0:LANES]
            for t in range(1, tk // LANES):
                psum = psum + pr[:, t * LANES:(t + 1) * LANES]
            l_ref[m, rows, :] = alpha * l_ref[m, rows, :] + psum
            acc_ref[m, rows, :] = (jnp.concatenate([alpha] * (2 * dh // LANES), axis=1) * acc_ref[m, rows, :]
                                   + jnp.dot(pr.astype(BF16), v, preferred_element_type=F32))
            m_ref[m, rows, :] = m_new

    d = ki - qi * parts
    pl.when(d < 0)(functools.partial(update, None))
    for j in range(parts):
        pl.when(d == j)(functools.partial(update, j))

    @pl.when(d == parts - 1)
    def _():
        lam = (jnp.exp(jnp.sum(lam_ref[0:1, :] * lam_ref[1:2, :], axis=-1, keepdims=True))
               - jnp.exp(jnp.sum(lam_ref[2:3, :] * lam_ref[3:4, :], axis=-1, keepdims=True))
               + lambda_init)
        inv_l0 = 1.0 / jnp.sum(l_ref[0], axis=-1, keepdims=True)
        inv_l1 = lam / jnp.sum(l_ref[1], axis=-1, keepdims=True)
        a = acc_ref[0] * inv_l0 - acc_ref[1] * inv_l1
        o_ref[...] = (_rms_scale(a, sw_ref[...]) * (1.0 - lambda_init)).astype(o_ref.dtype)


def diff_attention(q, kv, lam_rows, subln_w, ffn_w_up, ffn_w_down, *, layer, batch, seq, tq, tk, lambda_init,
                   q_col0):
    t = q.shape[0]
    hw = 2 * DIFF_DH
    nq = seq // tq
    nk = seq // tk
    nslices = batch * DIFF_HEADS
    _, up_rows, up_cols = ffn_w_up.shape
    _, dn_rows, dn_cols = ffn_w_down.shape
    pairs = [(a, b) for a in range(nq) for b in range(nk) if b * tk < (a + 1) * tq]
    qi_tab = jnp.asarray([a for a, _ in pairs], jnp.int32)
    ki_tab = jnp.asarray([b for _, b in pairs], jnp.int32)
    kern = functools.partial(_attn_kernel, tq=tq, tk=tk, lambda_init=lambda_init)
    grid_spec = pltpu.PrefetchScalarGridSpec(
        num_scalar_prefetch=2,
        grid=(batch, DIFF_HEADS, len(pairs)),
        in_specs=[
            pl.BlockSpec((tq, hw), lambda b, h, p, qt, kt: (b * nq + qt[p], q_col0 + h)),
            pl.BlockSpec((tk, hw), lambda b, h, p, qt, kt: (b * nk + kt[p], h)),
            pl.BlockSpec((tk, hw), lambda b, h, p, qt, kt: (b * nk + kt[p], DIFF_HEADS + h)),
            pl.BlockSpec((SUBLANES, DIFF_DH), lambda b, h, p, qt, kt: (0, 0)),
            pl.BlockSpec((1, hw), lambda b, h, p, qt, kt: (0, 0)),
            pl.BlockSpec((None, up_rows // nslices, up_cols),
                         lambda b, h, p, qt, kt: (layer, b * DIFF_HEADS + h, 0)),
            pl.BlockSpec((None, dn_rows // nslices, dn_cols),
                         lambda b, h, p, qt, kt: (layer, b * DIFF_HEADS + h, 0)),
        ],
        out_specs=[
            pl.BlockSpec((tq, hw), lambda b, h, p, qt, kt: (b * nq + qt[p], h)),
            pl.BlockSpec((up_rows // nslices, up_cols), lambda b, h, p, qt, kt: (b * DIFF_HEADS + h, 0)),
            pl.BlockSpec((dn_rows // nslices, dn_cols), lambda b, h, p, qt, kt: (b * DIFF_HEADS + h, 0)),
        ],
        scratch_shapes=[
            pltpu.VMEM((2, tq, LANES), F32),
            pltpu.VMEM((2, tq, LANES), F32),
            pltpu.VMEM((2, tq, hw), F32),
        ],
    )
    return pl.pallas_call(
        kern,
        grid_spec=grid_spec,
        out_shape=[
            jax.ShapeDtypeStruct((t, DIFF_HEADS * hw), BF16),
            jax.ShapeDtypeStruct((up_rows, up_cols), BF16),
            jax.ShapeDtypeStruct((dn_rows, dn_cols), BF16),
        ],
        compiler_params=_params("parallel", "parallel", "arbitrary"),
        name="diff_attention",
    )(qi_tab, ki_tab, q, kv, kv, lam_rows, subln_w.reshape(1, hw), ffn_w_up, ffn_w_down)


def kernel(x, norm_mix_pre, norm_mix_post, norm_ffn_pre, norm_ffn_post, gdn_w_in, gdn_conv_w, gdn_a_log,
           gdn_dt_bias, gdn_norm_w, gdn_w_out, kv_norm_w, w_kv, diff_w_q, diff_lq1, diff_lk1, diff_lq2,
           diff_lk2, diff_subln_w, diff_w_o, ffn_w_up, ffn_conv_w, ffn_conv_b, ffn_w_down):
    batch, seq, d = x.shape
    t = batch * seq
    h = x.reshape(t, d)
    n_main = 2 * GDN_QK_DIM + 2 * GDN_V_DIM

    for layer in range(DEPTH):
        if layer < N_A:
            w_in = gdn_w_in[layer].astype(BF16)
            w_ba = jnp.pad(w_in[:, n_main:], ((0, 0), (0, LANES - 2 * GDN_HEADS)))
            proj, ba = gdn_in_proj(h, norm_mix_pre[layer], w_in, w_ba, gdn_conv_w[layer], seq=seq, tm=1024,
                                   tn=512, row_parts=4)
            prm = jnp.zeros((SUBLANES, LANES), F32)
            prm = prm.at[0, GDN_HEADS:2 * GDN_HEADS].set(gdn_a_log[layer].astype(F32))
            prm = prm.at[1, GDN_HEADS:2 * GDN_HEADS].set(gdn_dt_bias[layer].astype(F32))
            gates, gates_t = gdn_gates(ba, prm, chunks_per_step=4)
            o, w_up_b, w_down_b = gdn_core(proj, gates, gates_t, gdn_norm_w[layer], ffn_w_up, ffn_w_down,
                                           layer=layer, batch=batch, seq=seq, hb=8, ts=1024, chunks_a=2)
            h = matmul_norm_res(o, gdn_w_out[layer].astype(BF16), h, norm_mix_post[layer], tm=512)
        else:
            j = layer - N_A
            lambda_init = 0.8 - 0.6 * math.exp(-0.3 * layer)
            q_scale = DIFF_DH ** -0.5 * math.log2(math.e)
            tn = 512
            if layer == N_A:
                w_kvq = jnp.concatenate([w_kv, diff_w_q[j]], axis=1).astype(BF16)
                kv = norm_matmul(h, jnp.stack([kv_norm_w, norm_mix_pre[layer]]), w_kvq, BF16, tm=1024, tn=tn,
                                 split=w_kv.shape[1] // tn, out_scale=q_scale)
                q, q_col0 = kv, w_kv.shape[1] // (2 * DIFF_DH)
            else:
                q = norm_matmul(h, norm_mix_pre[layer][None], diff_w_q[j].astype(BF16), BF16, tm=1024, tn=tn,
                                out_scale=q_scale)
                q_col0 = 0
            lam_rows = jnp.zeros((SUBLANES, DIFF_DH), F32)
            lam_rows = lam_rows.at[0].set(diff_lq1[j]).at[1].set(diff_lk1[j])
            lam_rows = lam_rows.at[2].set(diff_lq2[j]).at[3].set(diff_lk2[j])
            att, w_up_b, w_down_b = diff_attention(q, kv, lam_rows, diff_subln_w[j], ffn_w_up, ffn_w_down,
                                                   layer=layer, batch=batch, seq=seq, tq=2048, tk=512,
                                                   lambda_init=lambda_init, q_col0=q_col0)
            h = matmul_norm_res(att, diff_w_o[j].astype(BF16), h, norm_mix_post[layer], tm=512)
        h = conv_ffn(h, norm_ffn_pre[layer], w_up_b, ffn_conv_w[layer], ffn_conv_b[layer], w_down_b,
                     norm_ffn_post[layer], seq=seq, tm=512, tf=512)
    return h.reshape(batch, seq, d)
```

```python
import functools
import math

import jax
import jax.numpy as jnp
from jax import lax
from jax.experimental import pallas as pl
from jax.experimental.pallas import tpu as pltpu

D_MODEL = 2048
DEPTH = 2
N_A = DEPTH // 2
CHUNK = 64
GDN_HEADS = 16
GDN_DK = 128
GDN_DV = 128
GDN_CONV = 4
GDN_QK_DIM = GDN_HEADS * GDN_DK
GDN_V_DIM = GDN_HEADS * GDN_DV
DIFF_HEADS = 8
DIFF_DH = 128
DIFF_QK_DIM = DIFF_HEADS * 2 * DIFF_DH
D_FF = 5632
FFN_CONV = 3
EPS = 1e-6

LANES = 128
SUBLANES = 8
VMEM_LIMIT_BYTES = 56 * 1024 * 1024

DELTA_CHUNK = 128
INV_BASE = 16

F32 = jnp.float32
BF16 = jnp.bfloat16
NT_DIMS = (((1,), (1,)), ((), ()))
TN_DIMS = (((0,), (0,)), ((), ()))


def _bdot(a, b):
    return jnp.dot(a.astype(BF16), b.astype(BF16), preferred_element_type=F32)


def _rms_scale(x, gain):
    ms = jnp.mean(x * x, axis=-1, keepdims=True)
    return x * lax.rsqrt(ms + EPS) * gain


def _silu(x):
    return x * jax.nn.sigmoid(x)


def _params(*sem):
    return pltpu.CompilerParams(dimension_semantics=sem, vmem_limit_bytes=VMEM_LIMIT_BYTES)


def _causal_delay(y, prev, shift):
    head = jnp.concatenate([prev, y[0:SUBLANES, :]], axis=0)
    first = pltpu.roll(head, shift, axis=0)[SUBLANES:, :]
    body = pltpu.roll(y, shift, axis=0)
    return jnp.concatenate([first, body[SUBLANES:, :]], axis=0)


def _norm_matmul_kernel(x_ref, g_ref, w_ref, o_ref, xn_ref, *, split, out_scale):
    j = pl.program_id(1)
    nseg = g_ref.shape[0]

    @pl.when(j == 0)
    def _():
        x = x_ref[...]
        xh = x * lax.rsqrt(jnp.mean(x * x, axis=-1, keepdims=True) + EPS)
        for s in range(nseg):
            xn_ref[s] = (xh * g_ref[s:s + 1, :]).astype(BF16)

    if nseg == 1:
        y = jnp.dot(xn_ref[0], w_ref[...], preferred_element_type=F32)
        if out_scale is not None:
            y = y * out_scale
    else:
        second = j >= split
        y = jnp.dot(xn_ref[second.astype(jnp.int32)], w_ref[...], preferred_element_type=F32)
        if out_scale is not None:
            y = y * jnp.where(second, out_scale, 1.0)
    o_ref[...] = y.astype(o_ref.dtype)


def norm_matmul(x, gains, w, out_dtype, *, tm, tn, split=None, out_scale=None):
    m, k = x.shape
    n = w.shape[1]
    nseg = gains.shape[0]
    return pl.pallas_call(
        functools.partial(_norm_matmul_kernel, split=split, out_scale=out_scale),
        grid=(m // tm, n // tn),
        in_specs=[
            pl.BlockSpec((tm, k), lambda i, j: (i, 0)),
            pl.BlockSpec((nseg, k), lambda i, j: (0, 0)),
            pl.BlockSpec((k, tn), lambda i, j: (0, j)),
        ],
        out_specs=pl.BlockSpec((tm, tn), lambda i, j: (i, j)),
        out_shape=jax.ShapeDtypeStruct((m, n), out_dtype),
        scratch_shapes=[pltpu.VMEM((nseg, tm, k), BF16)],
        compiler_params=_params("parallel", "arbitrary"),
        name="norm_matmul",
    )(x, gains, w)


def _gdn_in_proj_kernel(x_ref, g_ref, w_ref, wba_ref, cw_ref, o_ref, ba_ref, xn_ref, carry_ref, *,
                        tm, row_parts, tiles_per_seq, n_q_tiles, n_norm_tiles, n_conv_tiles):
    i = pl.program_id(0)
    j = pl.program_id(1)
    tn = w_ref.shape[1]

    @pl.when(j == 0)
    def _():
        xn_ref[...] = _rms_scale(x_ref[...], g_ref[...]).astype(BF16)
        ba_ref[...] = jnp.dot(xn_ref[...], wba_ref[...], preferred_element_type=F32)

    rp = tm // row_parts

    def project():
        return [jnp.dot(xn_ref[r * rp:(r + 1) * rp, :], w_ref[...], preferred_element_type=F32)
                for r in range(row_parts)]

    def conv_silu(y, prev):
        acc = None
        for tap in range(GDN_CONV):
            delay = GDN_CONV - 1 - tap
            yd = y if delay == 0 else _causal_delay(y, prev, delay)
            term = yd * cw_ref[tap:tap + 1, :]
            acc = term if acc is None else acc + term
        return _silu(acc)

    def epilogue(normalize):
        ys = project()
        seq_start = (i % tiles_per_seq) == 0
        prev = jnp.where(seq_start, 0.0, carry_ref[j])
        carry_ref[j] = ys[-1][rp - SUBLANES:, :]
        scale = jnp.where(j < n_q_tiles, GDN_DK ** -0.5, 1.0)
        for r, y in enumerate(ys):
            act = conv_silu(y, prev)
            prev = y[rp - SUBLANES:, :]
            if normalize:
                heads = []
                for c0 in range(0, tn, GDN_DK):
                    a = act[:, c0:c0 + GDN_DK]
                    heads.append(a * (lax.rsqrt(jnp.sum(a * a, axis=-1, keepdims=True) + EPS) * scale))
                act = jnp.concatenate(heads, axis=1)
            o_ref[r * rp:(r + 1) * rp, :] = act.astype(o_ref.dtype)

    pl.when(j < n_norm_tiles)(functools.partial(epilogue, True))
    pl.when(jnp.logical_and(j >= n_norm_tiles, j < n_conv_tiles))(functools.partial(epilogue, False))

    @pl.when(j >= n_conv_tiles)
    def _():
        for r, y in enumerate(project()):
            o_ref[r * rp:(r + 1) * rp, :] = y.astype(o_ref.dtype)


def gdn_in_proj(x, gain, w_in, w_ba, conv_w, *, seq, tm, tn, row_parts):
    m, k = x.shape
    n_main = 2 * GDN_QK_DIM + 2 * GDN_V_DIM
    n_conv_tiles = (2 * GDN_QK_DIM + GDN_V_DIM) // tn
    kern = functools.partial(
        _gdn_in_proj_kernel, tm=tm, row_parts=row_parts, tiles_per_seq=seq // tm, n_q_tiles=GDN_QK_DIM // tn,
        n_norm_tiles=2 * GDN_QK_DIM // tn, n_conv_tiles=n_conv_tiles)
    return pl.pallas_call(
        kern,
        grid=(m // tm, n_main // tn),
        in_specs=[
            pl.BlockSpec((tm, k), lambda i, j: (i, 0)),
            pl.BlockSpec((1, k), lambda i, j: (0, 0)),
            pl.BlockSpec((k, tn), lambda i, j: (0, j)),
            pl.BlockSpec((k, LANES), lambda i, j: (0, 0)),
            pl.BlockSpec((GDN_CONV, tn), lambda i, j: (0, jnp.minimum(j, n_conv_tiles - 1))),
        ],
        out_specs=[
            pl.BlockSpec((tm, tn), lambda i, j: (i, j)),
            pl.BlockSpec((tm, LANES), lambda i, j: (i, 0)),
        ],
        out_shape=[
            jax.ShapeDtypeStruct((m, n_main), BF16),
            jax.ShapeDtypeStruct((m, LANES), F32),
        ],
        scratch_shapes=[
            pltpu.VMEM((tm, k), BF16),
            pltpu.VMEM((n_conv_tiles, SUBLANES, tn), F32),
        ],
        compiler_params=_params("arbitrary", "arbitrary"),
        name="gdn_in_proj",
    )(x, gain.reshape(1, k), w_in, w_ba, conv_w)


def _matmul_norm_res_kernel(*refs, side_groups):
    n_in, n_side = 4, sum(side_groups)
    a_ref, w_ref, h_ref, g_ref = refs[:n_in]
    o_ref = refs[n_in + n_side]
    _cast_side(refs[n_in:n_in + n_side], refs[n_in + n_side + 1:], side_groups)
    m = jnp.dot(a_ref[...], w_ref[...], preferred_element_type=F32)
    o_ref[...] = h_ref[...] + _rms_scale(m, g_ref[...])


def matmul_norm_res(a, w, h, gain, *, tm, side=()):
    m, k = a.shape
    n = w.shape[1]
    side_in, side_out, side_shapes, side_args = _side_specs(side, m // tm, lambda i: i)
    return pl.pallas_call(
        functools.partial(_matmul_norm_res_kernel, side_groups=tuple(len(g) for g in side)),
        grid=(m // tm,),
        in_specs=[
            pl.BlockSpec((tm, k), lambda i: (i, 0)),
            pl.BlockSpec((k, n), lambda i: (0, 0)),
            pl.BlockSpec((tm, n), lambda i: (i, 0)),
            pl.BlockSpec((1, n), lambda i: (0, 0)),
        ] + side_in,
        out_specs=[pl.BlockSpec((tm, n), lambda i: (i, 0))] + side_out,
        out_shape=[jax.ShapeDtypeStruct((m, n), F32)] + side_shapes,
        compiler_params=_params("parallel"),
        name="matmul_norm_res",
    )(a, w, h, gain.reshape(1, n), *side_args)


def _ffn_kernel(h_ref, gpre_ref, wg_ref, wu_ref, cw_ref, cb_ref, wd_ref, gpost_ref, o_ref,
                xn_ref, acc_ref, carry_ref, *, tm, tiles_per_seq):
    i = pl.program_id(0)
    f = pl.program_id(1)

    @pl.when(f == 0)
    def _():
        xn_ref[...] = _rms_scale(h_ref[...], gpre_ref[...]).astype(BF16)
        acc_ref[...] = jnp.zeros_like(acc_ref)

    xn = xn_ref[...]
    gate = jnp.dot(xn, wg_ref[...], preferred_element_type=F32)
    up = jnp.dot(xn, wu_ref[...], preferred_element_type=F32)

    seq_start = (i % tiles_per_seq) == 0
    prev = jnp.where(seq_start, 0.0, carry_ref[f])
    carry_ref[f] = gate[tm - SUBLANES:, :]
    conv = (_causal_delay(gate, prev, 2) * cw_ref[0:1, :] + _causal_delay(gate, prev, 1) * cw_ref[1:2, :]
            + gate * cw_ref[2:3, :] + cb_ref[...])
    act = (_silu(conv) * up).astype(BF16)
    acc_ref[...] += jnp.dot(act, wd_ref[...], preferred_element_type=F32)

    @pl.when(f == pl.num_programs(1) - 1)
    def _():
        o_ref[...] = h_ref[...] + _rms_scale(acc_ref[...], gpost_ref[...])


def conv_ffn(h, gpre, w_up, conv_w, conv_b, w_down, gpost, *, seq, tm, tf):
    m, d = h.shape
    dff = w_down.shape[0]
    nf = dff // tf
    kern = functools.partial(_ffn_kernel, tm=tm, tiles_per_seq=seq // tm)
    return pl.pallas_call(
        kern,
        grid=(m // tm, nf),
        in_specs=[
            pl.BlockSpec((tm, d), lambda i, f: (i, 0)),
            pl.BlockSpec((1, d), lambda i, f: (0, 0)),
            pl.BlockSpec((d, tf), lambda i, f: (0, f)),
            pl.BlockSpec((d, tf), lambda i, f: (0, f + nf)),
            pl.BlockSpec((FFN_CONV, tf), lambda i, f: (0, f)),
            pl.BlockSpec((1, tf), lambda i, f: (0, f)),
            pl.BlockSpec((tf, d), lambda i, f: (f, 0)),
            pl.BlockSpec((1, d), lambda i, f: (0, 0)),
        ],
        out_specs=pl.BlockSpec((tm, d), lambda i, f: (i, 0)),
        out_shape=jax.ShapeDtypeStruct((m, d), F32),
        scratch_shapes=[
            pltpu.VMEM((tm, d), BF16),
            pltpu.VMEM((tm, d), F32),
            pltpu.VMEM((nf, SUBLANES, tf), F32),
        ],
        compiler_params=_params("arbitrary", "arbitrary"),
        name="conv_ffn",
    )(h, gpre.reshape(1, d), w_up, w_up, conv_w, conv_b.reshape(1, dff), w_down, gpost.reshape(1, d))


def _side_specs(groups, nsteps, step_of):
    in_specs, out_specs, out_shapes, args = [], [], [], []
    for group in groups:
        rows = group[0][0].shape[-2]
        rps = rows // nsteps
        for arr, lead in group:
            cols = arr.shape[-1]
            if lead is None:
                in_specs.append(pl.BlockSpec((rps, cols), lambda *g: (step_of(*g), 0)))
            else:
                in_specs.append(pl.BlockSpec((None, rps, cols), lambda *g, lead=lead: (lead, step_of(*g), 0)))
            args.append(arr)
        total = sum(arr.shape[-1] for arr, _ in group)
        out_specs.append(pl.BlockSpec((rps, total), lambda *g: (step_of(*g), 0)))
        out_shapes.append(jax.ShapeDtypeStruct((rows, total), BF16))
    return in_specs, out_specs, out_shapes, args


def _cast_side(src_refs, dst_refs, group_sizes):
    k = 0
    for n, dst in zip(group_sizes, dst_refs):
        c0 = 0
        for src in src_refs[k:k + n]:
            dst[:, c0:c0 + src.shape[1]] = src[...].astype(BF16)
            c0 += src.shape[1]
        k += n


def _gates_kernel(ba_ref, prm_ref, g_ref, gt_ref):
    c = DELTA_CHUNK
    lane = lax.broadcasted_iota(jnp.int32, (c, LANES), 1)
    is_beta = lane < GDN_HEADS
    is_g = jnp.logical_and(lane >= GDN_HEADS, lane < 2 * GDN_HEADS)
    row = lax.broadcasted_iota(jnp.int32, (c, c), 0)
    col = lax.broadcasted_iota(jnp.int32, (c, c), 1)
    tril = (row >= col).astype(F32)
    for ci in range(gt_ref.shape[0]):
        ba = ba_ref[ci * c:(ci + 1) * c, :]
        g = -jnp.exp(prm_ref[0:1, :]) * jax.nn.softplus(ba + prm_ref[1:2, :])
        g = jnp.where(is_g, g, 0.0)
        gcum = jnp.dot(tril, g, preferred_element_type=F32, precision=lax.Precision.HIGHEST)
        out = jnp.where(is_beta, jax.nn.sigmoid(ba), gcum)
        g_ref[ci * c:(ci + 1) * c, :] = out
        gt_ref[ci] = out.T


def gdn_gates(ba, prm, *, chunks_per_step):
    t = ba.shape[0]
    c = DELTA_CHUNK
    rows = chunks_per_step * c
    return pl.pallas_call(
        _gates_kernel,
        grid=(t // rows,),
        in_specs=[
            pl.BlockSpec((rows, LANES), lambda i: (i, 0)),
            pl.BlockSpec((SUBLANES, LANES), lambda i: (0, 0)),
        ],
        out_specs=[
            pl.BlockSpec((rows, LANES), lambda i: (i, 0)),
            pl.BlockSpec((chunks_per_step, LANES, c), lambda i: (i, 0, 0)),
        ],
        out_shape=[
            jax.ShapeDtypeStruct((t, LANES), F32),
            jax.ShapeDtypeStruct((t // c, LANES, c), F32),
        ],
        compiler_params=_params("parallel"),
        name="gdn_gates",
    )(ba, prm)


def _unit_lower_inverse(lows, masks):
    eye, base_mask, merge_masks = masks
    a_s = [jnp.where(base_mask, -low, 0.0) for low in lows]
    xs = [eye + a for a in a_s]
    span = 2
    while span < INV_BASE:
        a_s = [_bdot(a, a) for a in a_s]
        xs = [x + _bdot(x, a) for x, a in zip(xs, a_s)]
        span *= 2
    for mm in merge_masks:
        ys = [_bdot(x, jnp.where(mm, low, 0.0)) for x, low in zip(xs, lows)]
        xs = [x - _bdot(y, x) for x, y in zip(xs, ys)]
    return xs


def _inverse_masks(c):
    row = lax.broadcasted_iota(jnp.int32, (c, c), 0)
    col = lax.broadcasted_iota(jnp.int32, (c, c), 1)
    eye = (row == col).astype(F32)
    same = lambda b: (row // b) == (col // b)
    base_mask = same(INV_BASE)
    merge_masks = []
    b = INV_BASE
    while b < c:
        merge_masks.append(jnp.logical_and(same(2 * b), jnp.logical_not(same(b))))
        b *= 2
    return eye, base_mask, merge_masks


def _gdn_kernel(*refs, hb, ts, chunks_a, side_groups):
    n_in, n_side = 7, sum(side_groups)
    q_ref, k_ref, v_ref, z_ref, g_ref, gt_ref, nw_ref = refs[:n_in]
    side_in = refs[n_in:n_in + n_side]
    o_ref = refs[n_in + n_side]
    side_out = refs[n_in + n_side + 1:n_in + n_side + 1 + len(side_groups)]
    state_ref, u_ref, wq_ref, kd_ref, ai_ref, eg_ref = refs[n_in + n_side + 1 + len(side_groups):]
    c = DELTA_CHUNK
    nchunk = ts // c
    hg = pl.program_id(1)
    tb = pl.program_id(2)

    _cast_side(side_in, side_out, side_groups)

    @pl.when(tb == 0)
    def _():
        state_ref[...] = jnp.zeros_like(state_ref)

    row = lax.broadcasted_iota(jnp.int32, (c, c), 0)
    col = lax.broadcasted_iota(jnp.int32, (c, c), 1)
    causal = row >= col
    strict = row > col
    inv_masks = _inverse_masks(c)
    lane = lax.broadcasted_iota(jnp.int32, (c, LANES), 1)

    def phase_a(it, carry):
        items = []
        for uu in range(chunks_a):
            ci = it * chunks_a + uu
            r0 = pl.multiple_of(ci * c, c)
            gates = g_ref[pl.ds(r0, c), :]
            for i in range(hb):
                head = hg * hb + i
                lanes = slice(i * LANES, (i + 1) * LANES)
                beta = jnp.sum(jnp.where(lane == head, gates, 0.0), axis=-1, keepdims=True)
                gc = jnp.sum(jnp.where(lane == head + GDN_HEADS, gates, 0.0), axis=-1, keepdims=True)
                gc_row = gt_ref[ci, pl.ds(head + GDN_HEADS, 1), :]
                items.append(dict(ci=ci, r0=r0, i=i, lanes=lanes, beta=beta, gc=gc, gc_row=gc_row,
                                  g_last=gc_row[:, c - 1:c]))
        for d in items:
            d["qn"] = q_ref[pl.ds(d["r0"], c), d["lanes"]]
            d["kn"] = k_ref[pl.ds(d["r0"], c), d["lanes"]]
            d["kb"] = d["kn"].astype(F32) * d["beta"]
        for d in items:
            d["qk"] = lax.dot_general(jnp.concatenate([d["qn"], d["kb"].astype(BF16)], axis=0), d["kn"],
                                      NT_DIMS, preferred_element_type=F32)
        lows = []
        for d in items:
            decay = jnp.exp(jnp.where(causal, d["gc"] - d["gc_row"], -jnp.inf))
            ai_ref[pl.ds(d["r0"], c), d["lanes"]] = jnp.where(causal, d["qk"][:c] * decay, 0.0).astype(BF16)
            lows.append(jnp.where(strict, d["qk"][c:] * decay, 0.0))
        tinvs = _unit_lower_inverse(lows, inv_masks)
        rhss = []
        for d in items:
            d["egc"] = jnp.exp(d["gc"])
            vv = v_ref[pl.ds(d["r0"], c), d["lanes"]].astype(F32)
            rhss.append(jnp.concatenate([vv * d["beta"], d["kb"] * d["egc"]], axis=1))
        sols = [_bdot(tinv, rhs) for tinv, rhs in zip(tinvs, rhss)]
        for d, sol in zip(items, sols):
            r0, lanes = d["r0"], d["lanes"]
            u_ref[pl.ds(r0, c), lanes] = sol[:, :LANES]
            wq_ref[0, pl.ds(r0, c), lanes] = sol[:, LANES:].astype(BF16)
            wq_ref[1, pl.ds(r0, c), lanes] = (d["qn"].astype(F32) * d["egc"]).astype(BF16)
            kd_ref[pl.ds(r0, c), lanes] = (d["kn"].astype(F32) * jnp.exp(d["g_last"] - d["gc"])).astype(BF16)
            eg_ref[d["ci"], d["i"]] = jnp.broadcast_to(jnp.exp(d["g_last"]), (SUBLANES, LANES))
        return carry

    lax.fori_loop(0, nchunk // chunks_a, phase_a, 0)

    def phase_b(ci, carry):
        r0 = pl.multiple_of(ci * c, c)
        heads = range(hb)
        lanes = [slice(i * LANES, (i + 1) * LANES) for i in heads]
        states = [state_ref[i] for i in heads]
        wss = [jnp.dot(jnp.concatenate([wq_ref[0, pl.ds(r0, c), lanes[i]], wq_ref[1, pl.ds(r0, c), lanes[i]]],
                                       axis=0),
                       states[i].astype(BF16), preferred_element_type=F32) for i in heads]
        v_news = [(u_ref[pl.ds(r0, c), lanes[i]] - wss[i][:c]).astype(BF16) for i in heads]
        upds = [lax.dot_general(kd_ref[pl.ds(r0, c), lanes[i]], v_news[i], TN_DIMS, preferred_element_type=F32)
                for i in heads]
        for i in heads:
            state_ref[i] = states[i] * eg_ref[ci, i][0:1, :] + upds[i]
        outs = [wss[i][c:] + jnp.dot(ai_ref[pl.ds(r0, c), lanes[i]], v_news[i], preferred_element_type=F32)
                for i in heads]
        for i in heads:
            z = z_ref[pl.ds(r0, c), lanes[i]].astype(F32)
            o_ref[pl.ds(r0, c), lanes[i]] = (_rms_scale(outs[i], nw_ref[...]) * _silu(z)).astype(o_ref.dtype)
        return carry

    lax.fori_loop(0, nchunk, phase_b, 0)


def gdn_core(proj, gates, gates_t, norm_w, side, *, batch, seq, hb, ts, chunks_a):
    t = proj.shape[0]
    w = hb * LANES
    ngroups = GDN_HEADS // hb
    nt = seq // ts
    nchunk = ts // DELTA_CHUNK
    side_in, side_out, side_shapes, side_args = _side_specs(
        side, batch * ngroups * nt, lambda b, h, s: (b * ngroups + h) * nt + s)
    kern = functools.partial(_gdn_kernel, hb=hb, ts=ts, chunks_a=chunks_a,
                             side_groups=tuple(len(g) for g in side))

    def col_spec(base):
        return pl.BlockSpec((ts, w), lambda b, h, s: (b * nt + s, base * ngroups + h))

    return pl.pallas_call(
        kern,
        grid=(batch, ngroups, nt),
        in_specs=[
            col_spec(0), col_spec(1), col_spec(2), col_spec(3),
            pl.BlockSpec((ts, LANES), lambda b, h, s: (b * nt + s, 0)),
            pl.BlockSpec((nchunk, LANES, DELTA_CHUNK), lambda b, h, s: (b * nt + s, 0, 0)),
            pl.BlockSpec((1, LANES), lambda b, h, s: (0, 0)),
        ] + side_in,
        out_specs=[pl.BlockSpec((ts, w), lambda b, h, s: (b * nt + s, h))] + side_out,
        out_shape=[jax.ShapeDtypeStruct((t, GDN_V_DIM), BF16)] + side_shapes,
        scratch_shapes=[
            pltpu.VMEM((hb, GDN_DK, GDN_DV), F32),
            pltpu.VMEM((ts, w), F32),
            pltpu.VMEM((2, ts, w), BF16),
            pltpu.VMEM((ts, w), BF16),
            pltpu.VMEM((ts, w), BF16),
            pltpu.VMEM((nchunk, hb, SUBLANES, LANES), F32),
        ],
        compiler_params=_params("parallel", "parallel", "arbitrary"),
        name="gdn_core",
    )(proj, proj, proj, proj, gates, gates_t, norm_w.reshape(1, LANES), *side_args)


def _attn_kernel(qi_tab, ki_tab, *refs, tq, tk, lambda_init, side_groups):
    n_in, n_side = 5, sum(side_groups)
    q_ref, k_ref, v_ref, lam_ref, sw_ref = refs[:n_in]
    side_in = refs[n_in:n_in + n_side]
    o_ref = refs[n_in + n_side]
    side_out = refs[n_in + n_side + 1:n_in + n_side + 1 + len(side_groups)]
    m_ref, l_ref, acc_ref = refs[n_in + n_side + 1 + len(side_groups):]
    p = pl.program_id(2)
    qi = qi_tab[p]
    ki = ki_tab[p]
    dh = DIFF_DH

    pl.when(p == 0)(functools.partial(_cast_side, side_in, side_out, side_groups))

    @pl.when(ki == 0)
    def _():
        m_ref[...] = jnp.full_like(m_ref, -jnp.inf)
        l_ref[...] = jnp.zeros_like(l_ref)
        acc_ref[...] = jnp.zeros_like(acc_ref)

    parts = tq // tk
    row = lax.broadcasted_iota(jnp.int32, (tk, tk), 0)
    col = lax.broadcasted_iota(jnp.int32, (tk, tk), 1)
    diag_visible = (col // CHUNK) <= (row // CHUNK)

    def update(diag_part):
        k = k_ref[...]
        v = v_ref[...]
        first = 0 if diag_part is None else diag_part
        chains = [(r, m) for r in range(first, parts) for m in range(2)]

        def scores(r, m):
            return lax.dot_general(q_ref[r * tk:(r + 1) * tk, m * dh:(m + 1) * dh], k[:, m * dh:(m + 1) * dh],
                                   NT_DIMS, preferred_element_type=F32)

        s_next = scores(*chains[0])
        for idx, (r, m) in enumerate(chains):
            s = s_next
            if idx + 1 < len(chains):
                s_next = scores(*chains[idx + 1])
            rows = slice(r * tk, (r + 1) * tk)
            if r == diag_part:
                s = jnp.where(diag_visible, s, -jnp.inf)
            m_old = m_ref[m, rows, :]
            m_new = jnp.maximum(m_old, jnp.max(s, axis=-1, keepdims=True))
            alpha = jnp.exp2(m_old - m_new)
            pr = jnp.exp2(s - jnp.concatenate([m_new] * (tk // LANES), axis=1))
            psum = pr[:, 0:LANES]
            for t in range(1, tk // LANES):
                psum = psum + pr[:, t * LANES:(t + 1) * LANES]
            l_ref[m, rows, :] = alpha * l_ref[m, rows, :] + psum
            acc_ref[m, rows, :] = (jnp.concatenate([alpha] * (2 * dh // LANES), axis=1) * acc_ref[m, rows, :]
                                   + jnp.dot(pr.astype(BF16), v, preferred_element_type=F32))
            m_ref[m, rows, :] = m_new

    d = ki - qi * parts
    pl.when(d < 0)(functools.partial(update, None))
    for j in range(parts):
        pl.when(d == j)(functools.partial(update, j))

    @pl.when(d == parts - 1)
    def _():
        lam = (jnp.exp(jnp.sum(lam_ref[0:1, :] * lam_ref[1:2, :], axis=-1, keepdims=True))
               - jnp.exp(jnp.sum(lam_ref[2:3, :] * lam_ref[3:4, :], axis=-1, keepdims=True))
               + lambda_init)
        inv_l0 = 1.0 / jnp.sum(l_ref[0], axis=-1, keepdims=True)
        inv_l1 = lam / jnp.sum(l_ref[1], axis=-1, keepdims=True)
        a = acc_ref[0] * inv_l0 - acc_ref[1] * inv_l1
        o_ref[...] = (_rms_scale(a, sw_ref[...]) * (1.0 - lambda_init)).astype(o_ref.dtype)


def diff_attention(q, kv, lam_rows, subln_w, side, *, batch, seq, tq, tk, lambda_init, q_col0):
    t = q.shape[0]
    hw = 2 * DIFF_DH
    nq = seq // tq
    nk = seq // tk
    side_in, side_out, side_shapes, side_args = _side_specs(
        side, batch * DIFF_HEADS, lambda b, h, p, qt, kt: b * DIFF_HEADS + h)
    pairs = [(a, b) for a in range(nq) for b in range(nk) if b * tk < (a + 1) * tq]
    qi_tab = jnp.asarray([a for a, _ in pairs], jnp.int32)
    ki_tab = jnp.asarray([b for _, b in pairs], jnp.int32)
    kern = functools.partial(_attn_kernel, tq=tq, tk=tk, lambda_init=lambda_init,
                             side_groups=tuple(len(g) for g in side))
    grid_spec = pltpu.PrefetchScalarGridSpec(
        num_scalar_prefetch=2,
        grid=(batch, DIFF_HEADS, len(pairs)),
        in_specs=[
            pl.BlockSpec((tq, hw), lambda b, h, p, qt, kt: (b * nq + qt[p], q_col0 + h)),
            pl.BlockSpec((tk, hw), lambda b, h, p, qt, kt: (b * nk + kt[p], h)),
            pl.BlockSpec((tk, hw), lambda b, h, p, qt, kt: (b * nk + kt[p], DIFF_HEADS + h)),
            pl.BlockSpec((SUBLANES, DIFF_DH), lambda b, h, p, qt, kt: (0, 0)),
            pl.BlockSpec((1, hw), lambda b, h, p, qt, kt: (0, 0)),
        ] + side_in,
        out_specs=[pl.BlockSpec((tq, hw), lambda b, h, p, qt, kt: (b * nq + qt[p], h))] + side_out,
        scratch_shapes=[
            pltpu.VMEM((2, tq, LANES), F32),
            pltpu.VMEM((2, tq, LANES), F32),
            pltpu.VMEM((2, tq, hw), F32),
        ],
    )
    return pl.pallas_call(
        kern,
        grid_spec=grid_spec,
        out_shape=[jax.ShapeDtypeStruct((t, DIFF_HEADS * hw), BF16)] + side_shapes,
        compiler_params=_params("parallel", "parallel", "arbitrary"),
        name="diff_attention",
    )(qi_tab, ki_tab, q, kv, kv, lam_rows, subln_w.reshape(1, hw), *side_args)


def kernel(x, norm_mix_pre, norm_mix_post, norm_ffn_pre, norm_ffn_post, gdn_w_in, gdn_conv_w, gdn_a_log,
           gdn_dt_bias, gdn_norm_w, gdn_w_out, kv_norm_w, w_kv, diff_w_q, diff_lq1, diff_lk1, diff_lq2,
           diff_lk2, diff_subln_w, diff_w_o, ffn_w_up, ffn_conv_w, ffn_conv_b, ffn_w_down):
    batch, seq, d = x.shape
    t = batch * seq
    h = x.reshape(t, d)
    n_main = 2 * GDN_QK_DIM + 2 * GDN_V_DIM

    for layer in range(DEPTH):
        if layer < N_A:
            w_in = gdn_w_in[layer].astype(BF16)
            w_ba = jnp.pad(w_in[:, n_main:], ((0, 0), (0, LANES - 2 * GDN_HEADS)))
            proj, ba = gdn_in_proj(h, norm_mix_pre[layer], w_in, w_ba, gdn_conv_w[layer], seq=seq, tm=1024,
                                   tn=512, row_parts=4)
            prm = jnp.zeros((SUBLANES, LANES), F32)
            prm = prm.at[0, GDN_HEADS:2 * GDN_HEADS].set(gdn_a_log[layer].astype(F32))
            prm = prm.at[1, GDN_HEADS:2 * GDN_HEADS].set(gdn_dt_bias[layer].astype(F32))
            gates, gates_t = gdn_gates(ba, prm, chunks_per_step=4)
            side = [[(ffn_w_up, layer)], [(ffn_w_down, layer)]]
            o, w_up_b, w_down_b = gdn_core(proj, gates, gates_t, gdn_norm_w[layer], side, batch=batch, seq=seq,
                                           hb=8, ts=1024, chunks_a=2)
            side = [[(w_kv, None), (diff_w_q, 0)]] if layer == N_A - 1 else []
            h, *rest = matmul_norm_res(o, gdn_w_out[layer].astype(BF16), h, norm_mix_post[layer], tm=512,
                                       side=side)
            if rest:
                w_kvq_b = rest[0]
        else:
            j = layer - N_A
            lambda_init = 0.8 - 0.6 * math.exp(-0.3 * layer)
            q_scale = DIFF_DH ** -0.5 * math.log2(math.e)
            tn = 512
            if layer == N_A:
                kv = norm_matmul(h, jnp.stack([kv_norm_w, norm_mix_pre[layer]]), w_kvq_b, BF16, tm=1024, tn=tn,
                                 split=w_kv.shape[1] // tn, out_scale=q_scale)
                q, q_col0 = kv, w_kv.shape[1] // (2 * DIFF_DH)
            else:
                q = norm_matmul(h, norm_mix_pre[layer][None], diff_w_q[j].astype(BF16), BF16, tm=1024, tn=tn,
                                out_scale=q_scale)
                q_col0 = 0
            lam_rows = jnp.zeros((SUBLANES, DIFF_DH), F32)
            lam_rows = lam_rows.at[0].set(diff_lq1[j]).at[1].set(diff_lk1[j])
            lam_rows = lam_rows.at[2].set(diff_lq2[j]).at[3].set(diff_lk2[j])
            side = [[(ffn_w_up, layer)], [(ffn_w_down, layer)], [(diff_w_o, j)]]
            att, w_up_b, w_down_b, w_o_b = diff_attention(q, kv, lam_rows, diff_subln_w[j], side, batch=batch,
                                                          seq=seq, tq=2048, tk=512, lambda_init=lambda_init,
                                                          q_col0=q_col0)
            h, = matmul_norm_res(att, w_o_b, h, norm_mix_post[layer], tm=512)
        h = conv_ffn(h, norm_ffn_pre[layer], w_up_b, ffn_conv_w[layer], ffn_conv_b[layer], w_down_b,
                     norm_ffn_post[layer], seq=seq, tm=512, tf=512)
    return h.reshape(batch, seq, d)
```

```python
import functools
import math

import jax
import jax.numpy as jnp
from jax import lax
from jax.experimental import pallas as pl
from jax.experimental.pallas import tpu as pltpu

D_MODEL = 2048
DEPTH = 2
N_A = DEPTH // 2
CHUNK = 64
GDN_HEADS = 16
GDN_DK = 128
GDN_DV = 128
GDN_CONV = 4
GDN_QK_DIM = GDN_HEADS * GDN_DK
GDN_V_DIM = GDN_HEADS * GDN_DV
DIFF_HEADS = 8
DIFF_DH = 128
DIFF_QK_DIM = DIFF_HEADS * 2 * DIFF_DH
D_FF = 5632
FFN_CONV = 3
EPS = 1e-6

LANES = 128
SUBLANES = 8
VMEM_LIMIT_BYTES = 56 * 1024 * 1024

DELTA_CHUNK = 128
INV_BASE = 16

F32 = jnp.float32
BF16 = jnp.bfloat16
NT_DIMS = (((1,), (1,)), ((), ()))
TN_DIMS = (((0,), (0,)), ((), ()))


def _bdot(a, b):
    return jnp.dot(a.astype(BF16), b.astype(BF16), preferred_element_type=F32)


def _rms_scale(x, gain):
    ms = jnp.mean(x * x, axis=-1, keepdims=True)
    return x * lax.rsqrt(ms + EPS) * gain


def _silu(x):
    return x * jax.nn.sigmoid(x)


def _params(*sem):
    return pltpu.CompilerParams(dimension_semantics=sem, vmem_limit_bytes=VMEM_LIMIT_BYTES)


def _causal_delay(y, prev, shift):
    head = jnp.concatenate([prev, y[0:SUBLANES, :]], axis=0)
    first = pltpu.roll(head, shift, axis=0)[SUBLANES:, :]
    body = pltpu.roll(y, shift, axis=0)
    return jnp.concatenate([first, body[SUBLANES:, :]], axis=0)


def _norm_matmul_kernel(x_ref, g_ref, w_ref, o_ref, xn_ref, *, split, out_scale):
    j = pl.program_id(1)
    nseg = g_ref.shape[0]

    @pl.when(j == 0)
    def _():
        x = x_ref[...]
        xh = x * lax.rsqrt(jnp.mean(x * x, axis=-1, keepdims=True) + EPS)
        for s in range(nseg):
            xn_ref[s] = (xh * g_ref[s:s + 1, :]).astype(BF16)

    if nseg == 1:
        y = jnp.dot(xn_ref[0], w_ref[...], preferred_element_type=F32)
        if out_scale is not None:
            y = y * out_scale
    else:
        second = j >= split
        y = jnp.dot(xn_ref[second.astype(jnp.int32)], w_ref[...], preferred_element_type=F32)
        if out_scale is not None:
            y = y * jnp.where(second, out_scale, 1.0)
    o_ref[...] = y.astype(o_ref.dtype)


def norm_matmul(x, gains, w, out_dtype, *, tm, tn, split=None, out_scale=None):
    m, k = x.shape
    n = w.shape[1]
    nseg = gains.shape[0]
    return pl.pallas_call(
        functools.partial(_norm_matmul_kernel, split=split, out_scale=out_scale),
        grid=(m // tm, n // tn),
        in_specs=[
            pl.BlockSpec((tm, k), lambda i, j: (i, 0)),
            pl.BlockSpec((nseg, k), lambda i, j: (0, 0)),
            pl.BlockSpec((k, tn), lambda i, j: (0, j)),
        ],
        out_specs=pl.BlockSpec((tm, tn), lambda i, j: (i, j)),
        out_shape=jax.ShapeDtypeStruct((m, n), out_dtype),
        scratch_shapes=[pltpu.VMEM((nseg, tm, k), BF16)],
        compiler_params=_params("parallel", "arbitrary"),
        name="norm_matmul",
    )(x, gains, w)


def _gdn_in_proj_kernel(x_ref, g_ref, w_ref, wba_ref, cw_ref, o_ref, ba_ref, xn_ref, carry_ref, *,
                        tm, row_parts, tiles_per_seq, n_q_tiles, n_norm_tiles, n_conv_tiles):
    i = pl.program_id(0)
    j = pl.program_id(1)
    tn = w_ref.shape[1]

    @pl.when(j == 0)
    def _():
        xn_ref[...] = _rms_scale(x_ref[...], g_ref[...]).astype(BF16)
        ba_ref[...] = jnp.dot(xn_ref[...], wba_ref[...], preferred_element_type=F32)

    rp = tm // row_parts

    def project():
        return [jnp.dot(xn_ref[r * rp:(r + 1) * rp, :], w_ref[...], preferred_element_type=F32)
                for r in range(row_parts)]

    def conv_silu(y, prev):
        acc = None
        for tap in range(GDN_CONV):
            delay = GDN_CONV - 1 - tap
            yd = y if delay == 0 else _causal_delay(y, prev, delay)
            term = yd * cw_ref[tap:tap + 1, :]
            acc = term if acc is None else acc + term
        return _silu(acc)

    def epilogue(normalize):
        ys = project()
        seq_start = (i % tiles_per_seq) == 0
        prev = jnp.where(seq_start, 0.0, carry_ref[j])
        carry_ref[j] = ys[-1][rp - SUBLANES:, :]
        scale = jnp.where(j < n_q_tiles, GDN_DK ** -0.5, 1.0)
        for r, y in enumerate(ys):
            act = conv_silu(y, prev)
            prev = y[rp - SUBLANES:, :]
            if normalize:
                heads = []
                for c0 in range(0, tn, GDN_DK):
                    a = act[:, c0:c0 + GDN_DK]
                    heads.append(a * (lax.rsqrt(jnp.sum(a * a, axis=-1, keepdims=True) + EPS) * scale))
                act = jnp.concatenate(heads, axis=1)
            o_ref[r * rp:(r + 1) * rp, :] = act.astype(o_ref.dtype)

    pl.when(j < n_norm_tiles)(functools.partial(epilogue, True))
    pl.when(jnp.logical_and(j >= n_norm_tiles, j < n_conv_tiles))(functools.partial(epilogue, False))

    @pl.when(j >= n_conv_tiles)
    def _():
        for r, y in enumerate(project()):
            o_ref[r * rp:(r + 1) * rp, :] = y.astype(o_ref.dtype)


def gdn_in_proj(x, gain, w_in, w_ba, conv_w, *, seq, tm, tn, row_parts):
    m, k = x.shape
    n_main = 2 * GDN_QK_DIM + 2 * GDN_V_DIM
    n_conv_tiles = (2 * GDN_QK_DIM + GDN_V_DIM) // tn
    kern = functools.partial(
        _gdn_in_proj_kernel, tm=tm, row_parts=row_parts, tiles_per_seq=seq // tm, n_q_tiles=GDN_QK_DIM // tn,
        n_norm_tiles=2 * GDN_QK_DIM // tn, n_conv_tiles=n_conv_tiles)
    return pl.pallas_call(
        kern,
        grid=(m // tm, n_main // tn),
        in_specs=[
            pl.BlockSpec((tm, k), lambda i, j: (i, 0)),
            pl.BlockSpec((1, k), lambda i, j: (0, 0)),
            pl.BlockSpec((k, tn), lambda i, j: (0, j)),
            pl.BlockSpec((k, LANES), lambda i, j: (0, 0)),
            pl.BlockSpec((GDN_CONV, tn), lambda i, j: (0, jnp.minimum(j, n_conv_tiles - 1))),
        ],
        out_specs=[
            pl.BlockSpec((tm, tn), lambda i, j: (i, j)),
            pl.BlockSpec((tm, LANES), lambda i, j: (i, 0)),
        ],
        out_shape=[
            jax.ShapeDtypeStruct((m, n_main), BF16),
            jax.ShapeDtypeStruct((m, LANES), F32),
        ],
        scratch_shapes=[
            pltpu.VMEM((tm, k), BF16),
            pltpu.VMEM((n_conv_tiles, SUBLANES, tn), F32),
        ],
        compiler_params=_params("arbitrary", "arbitrary"),
        name="gdn_in_proj",
    )(x, gain.reshape(1, k), w_in, w_ba, conv_w)


def _matmul_norm_res_kernel(*refs, side_groups):
    n_in, n_side = 4, sum(side_groups)
    a_ref, w_ref, h_ref, g_ref = refs[:n_in]
    o_ref = refs[n_in + n_side]
    _cast_side(refs[n_in:n_in + n_side], refs[n_in + n_side + 1:], side_groups)
    m = jnp.dot(a_ref[...], w_ref[...], preferred_element_type=F32)
    o_ref[...] = h_ref[...] + _rms_scale(m, g_ref[...])


def matmul_norm_res(a, w, h, gain, *, tm, side=()):
    m, k = a.shape
    n = w.shape[1]
    side_in, side_out, side_shapes, side_args = _side_specs(side, m // tm, lambda i: i)
    return pl.pallas_call(
        functools.partial(_matmul_norm_res_kernel, side_groups=tuple(len(g) for g in side)),
        grid=(m // tm,),
        in_specs=[
            pl.BlockSpec((tm, k), lambda i: (i, 0)),
            pl.BlockSpec((k, n), lambda i: (0, 0)),
            pl.BlockSpec((tm, n), lambda i: (i, 0)),
            pl.BlockSpec((1, n), lambda i: (0, 0)),
        ] + side_in,
        out_specs=[pl.BlockSpec((tm, n), lambda i: (i, 0))] + side_out,
        out_shape=[jax.ShapeDtypeStruct((m, n), F32)] + side_shapes,
        compiler_params=_params("parallel"),
        name="matmul_norm_res",
    )(a, w, h, gain.reshape(1, n), *side_args)


def _ffn_kernel(h_ref, gpre_ref, wg_ref, wu_ref, cw_ref, cb_ref, wd_ref, gpost_ref, o_ref,
                xn_ref, acc_ref, carry_ref, *, tm, tiles_per_seq):
    i = pl.program_id(0)
    f = pl.program_id(1)

    @pl.when(f == 0)
    def _():
        xn_ref[...] = _rms_scale(h_ref[...], gpre_ref[...]).astype(BF16)
        acc_ref[...] = jnp.zeros_like(acc_ref)

    xn = xn_ref[...]
    gate = jnp.dot(xn, wg_ref[...], preferred_element_type=F32)
    up = jnp.dot(xn, wu_ref[...], preferred_element_type=F32)

    seq_start = (i % tiles_per_seq) == 0
    prev = jnp.where(seq_start, 0.0, carry_ref[f])
    carry_ref[f] = gate[tm - SUBLANES:, :]
    conv = (_causal_delay(gate, prev, 2) * cw_ref[0:1, :] + _causal_delay(gate, prev, 1) * cw_ref[1:2, :]
            + gate * cw_ref[2:3, :] + cb_ref[...])
    act = (_silu(conv) * up).astype(BF16)
    acc_ref[...] += jnp.dot(act, wd_ref[...], preferred_element_type=F32)

    @pl.when(f == pl.num_programs(1) - 1)
    def _():
        o_ref[...] = h_ref[...] + _rms_scale(acc_ref[...], gpost_ref[...])


def conv_ffn(h, gpre, w_up, conv_w, conv_b, w_down, gpost, *, seq, tm, tf):
    m, d = h.shape
    dff = w_down.shape[0]
    nf = dff // tf
    kern = functools.partial(_ffn_kernel, tm=tm, tiles_per_seq=seq // tm)
    return pl.pallas_call(
        kern,
        grid=(m // tm, nf),
        in_specs=[
            pl.BlockSpec((tm, d), lambda i, f: (i, 0)),
            pl.BlockSpec((1, d), lambda i, f: (0, 0)),
            pl.BlockSpec((d, tf), lambda i, f: (0, f)),
            pl.BlockSpec((d, tf), lambda i, f: (0, f + nf)),
            pl.BlockSpec((FFN_CONV, tf), lambda i, f: (0, f)),
            pl.BlockSpec((1, tf), lambda i, f: (0, f)),
            pl.BlockSpec((tf, d), lambda i, f: (f, 0)),
            pl.BlockSpec((1, d), lambda i, f: (0, 0)),
        ],
        out_specs=pl.BlockSpec((tm, d), lambda i, f: (i, 0)),
        out_shape=jax.ShapeDtypeStruct((m, d), F32),
        scratch_shapes=[
            pltpu.VMEM((tm, d), BF16),
            pltpu.VMEM((tm, d), F32),
            pltpu.VMEM((nf, SUBLANES, tf), F32),
        ],
        compiler_params=_params("arbitrary", "arbitrary"),
        name="conv_ffn",
    )(h, gpre.reshape(1, d), w_up, w_up, conv_w, conv_b.reshape(1, dff), w_down, gpost.reshape(1, d))


def _side_specs(groups, nsteps, step_of):
    in_specs, out_specs, out_shapes, args = [], [], [], []
    for group in groups:
        rows = group[0][0].shape[-2]
        rps = rows // nsteps
        for arr, lead in group:
            cols = arr.shape[-1]
            if lead is None:
                in_specs.append(pl.BlockSpec((rps, cols), lambda *g: (step_of(*g), 0)))
            else:
                in_specs.append(pl.BlockSpec((None, rps, cols), lambda *g, lead=lead: (lead, step_of(*g), 0)))
            args.append(arr)
        total = sum(arr.shape[-1] for arr, _ in group)
        out_specs.append(pl.BlockSpec((rps, total), lambda *g: (step_of(*g), 0)))
        out_shapes.append(jax.ShapeDtypeStruct((rows, total), BF16))
    return in_specs, out_specs, out_shapes, args


def _cast_side(src_refs, dst_refs, group_sizes):
    k = 0
    for n, dst in zip(group_sizes, dst_refs):
        c0 = 0
        for src in src_refs[k:k + n]:
            dst[:, c0:c0 + src.shape[1]] = src[...].astype(BF16)
            c0 += src.shape[1]
        k += n


def _gates_kernel(ba_ref, prm_ref, g_ref, gt_ref):
    c = DELTA_CHUNK
    lane = lax.broadcasted_iota(jnp.int32, (c, LANES), 1)
    is_beta = lane < GDN_HEADS
    is_g = jnp.logical_and(lane >= GDN_HEADS, lane < 2 * GDN_HEADS)
    row = lax.broadcasted_iota(jnp.int32, (c, c), 0)
    col = lax.broadcasted_iota(jnp.int32, (c, c), 1)
    tril = (row >= col).astype(F32)
    for ci in range(gt_ref.shape[0]):
        ba = ba_ref[ci * c:(ci + 1) * c, :]
        g = -jnp.exp(prm_ref[0:1, :]) * jax.nn.softplus(ba + prm_ref[1:2, :])
        g = jnp.where(is_g, g, 0.0)
        gcum = jnp.dot(tril, g, preferred_element_type=F32, precision=lax.Precision.HIGHEST)
        out = jnp.where(is_beta, jax.nn.sigmoid(ba), gcum)
        g_ref[ci * c:(ci + 1) * c, :] = out
        gt_ref[ci] = out.T


def gdn_gates(ba, prm, *, chunks_per_step):
    t = ba.shape[0]
    c = DELTA_CHUNK
    rows = chunks_per_step * c
    return pl.pallas_call(
        _gates_kernel,
        grid=(t // rows,),
        in_specs=[
            pl.BlockSpec((rows, LANES), lambda i: (i, 0)),
            pl.BlockSpec((SUBLANES, LANES), lambda i: (0, 0)),
        ],
        out_specs=[
            pl.BlockSpec((rows, LANES), lambda i: (i, 0)),
            pl.BlockSpec((chunks_per_step, LANES, c), lambda i: (i, 0, 0)),
        ],
        out_shape=[
            jax.ShapeDtypeStruct((t, LANES), F32),
            jax.ShapeDtypeStruct((t // c, LANES, c), F32),
        ],
        compiler_params=_params("parallel"),
        name="gdn_gates",
    )(ba, prm)


def _unit_lower_inverse(lows, masks):
    eye, base_mask, merge_masks = masks
    a_s = [jnp.where(base_mask, -low, 0.0) for low in lows]
    xs = [eye + a for a in a_s]
    span = 2
    while span < INV_BASE:
        a_s = [_bdot(a, a) for a in a_s]
        xs = [x + _bdot(x, a) for x, a in zip(xs, a_s)]
        span *= 2
    for mm in merge_masks:
        ys = [_bdot(x, jnp.where(mm, low, 0.0)) for x, low in zip(xs, lows)]
        xs = [x - _bdot(y, x) for x, y in zip(xs, ys)]
    return xs


def _inverse_masks(c):
    row = lax.broadcasted_iota(jnp.int32, (c, c), 0)
    col = lax.broadcasted_iota(jnp.int32, (c, c), 1)
    eye = (row == col).astype(F32)
    same = lambda b: (row // b) == (col // b)
    base_mask = same(INV_BASE)
    merge_masks = []
    b = INV_BASE
    while b < c:
        merge_masks.append(jnp.logical_and(same(2 * b), jnp.logical_not(same(b))))
        b *= 2
    return eye, base_mask, merge_masks


def _gdn_kernel(*refs, hb, ts, chunks_a, side_groups):
    n_in, n_side = 7, sum(side_groups)
    q_ref, k_ref, v_ref, z_ref, g_ref, gt_ref, nw_ref = refs[:n_in]
    side_in = refs[n_in:n_in + n_side]
    o_ref = refs[n_in + n_side]
    side_out = refs[n_in + n_side + 1:n_in + n_side + 1 + len(side_groups)]
    state_ref, u_ref, wq_ref, kd_ref, ai_ref, eg_ref = refs[n_in + n_side + 1 + len(side_groups):]
    c = DELTA_CHUNK
    nchunk = ts // c
    hg = pl.program_id(1)
    tb = pl.program_id(2)

    _cast_side(side_in, side_out, side_groups)

    @pl.when(tb == 0)
    def _():
        state_ref[...] = jnp.zeros_like(state_ref)

    row = lax.broadcasted_iota(jnp.int32, (c, c), 0)
    col = lax.broadcasted_iota(jnp.int32, (c, c), 1)
    causal = row >= col
    strict = row > col
    inv_masks = _inverse_masks(c)
    lane = lax.broadcasted_iota(jnp.int32, (c, LANES), 1)

    def phase_a(it, carry):
        items = []
        for uu in range(chunks_a):
            ci = it * chunks_a + uu
            r0 = pl.multiple_of(ci * c, c)
            gates = g_ref[pl.ds(r0, c), :]
            for i in range(hb):
                head = hg * hb + i
                lanes = slice(i * LANES, (i + 1) * LANES)
                beta = jnp.sum(jnp.where(lane == head, gates, 0.0), axis=-1, keepdims=True)
                gc = jnp.sum(jnp.where(lane == head + GDN_HEADS, gates, 0.0), axis=-1, keepdims=True)
                gc_row = gt_ref[ci, pl.ds(head + GDN_HEADS, 1), :]
                items.append(dict(ci=ci, r0=r0, i=i, lanes=lanes, beta=beta, gc=gc, gc_row=gc_row,
                                  g_last=gc_row[:, c - 1:c]))
        for d in items:
            d["qn"] = q_ref[pl.ds(d["r0"], c), d["lanes"]]
            d["kn"] = k_ref[pl.ds(d["r0"], c), d["lanes"]]
            d["kb"] = d["kn"].astype(F32) * d["beta"]
        for d in items:
            d["qk"] = lax.dot_general(jnp.concatenate([d["qn"], d["kb"].astype(BF16)], axis=0), d["kn"],
                                      NT_DIMS, preferred_element_type=F32)
        lows = []
        for d in items:
            decay = jnp.exp(jnp.where(causal, d["gc"] - d["gc_row"], -jnp.inf))
            ai_ref[pl.ds(d["r0"], c), d["lanes"]] = jnp.where(causal, d["qk"][:c] * decay, 0.0).astype(BF16)
            lows.append(jnp.where(strict, d["qk"][c:] * decay, 0.0))
        tinvs = _unit_lower_inverse(lows, inv_masks)
        rhss = []
        for d in items:
            d["egc"] = jnp.exp(d["gc"])
            vv = v_ref[pl.ds(d["r0"], c), d["lanes"]].astype(F32)
            rhss.append(jnp.concatenate([vv * d["beta"], d["kb"] * d["egc"]], axis=1))
        sols = [_bdot(tinv, rhs) for tinv, rhs in zip(tinvs, rhss)]
        for d, sol in zip(items, sols):
            r0, lanes = d["r0"], d["lanes"]
            u_ref[pl.ds(r0, c), lanes] = sol[:, :LANES]
            wq_ref[0, pl.ds(r0, c), lanes] = sol[:, LANES:].astype(BF16)
            wq_ref[1, pl.ds(r0, c), lanes] = (d["qn"].astype(F32) * d["egc"]).astype(BF16)
            kd_ref[pl.ds(r0, c), lanes] = (d["kn"].astype(F32) * jnp.exp(d["g_last"] - d["gc"])).astype(BF16)
            eg_ref[d["ci"], d["i"]] = jnp.broadcast_to(jnp.exp(d["g_last"]), (SUBLANES, LANES))
        return carry

    lax.fori_loop(0, nchunk // chunks_a, phase_a, 0)

    def phase_b(ci, carry):
        r0 = pl.multiple_of(ci * c, c)
        heads = range(hb)
        lanes = [slice(i * LANES, (i + 1) * LANES) for i in heads]
        states = [state_ref[i] for i in heads]
        wss = [jnp.dot(jnp.concatenate([wq_ref[0, pl.ds(r0, c), lanes[i]], wq_ref[1, pl.ds(r0, c), lanes[i]]],
                                       axis=0),
                       states[i].astype(BF16), preferred_element_type=F32) for i in heads]
        v_news = [(u_ref[pl.ds(r0, c), lanes[i]] - wss[i][:c]).astype(BF16) for i in heads]
        upds = [lax.dot_general(kd_ref[pl.ds(r0, c), lanes[i]], v_news[i], TN_DIMS, preferred_element_type=F32)
                for i in heads]
        for i in heads:
            state_ref[i] = states[i] * eg_ref[ci, i][0:1, :] + upds[i]
        outs = [wss[i][c:] + jnp.dot(ai_ref[pl.ds(r0, c), lanes[i]], v_news[i], preferred_element_type=F32)
                for i in heads]
        for i in heads:
            z = z_ref[pl.ds(r0, c), lanes[i]].astype(F32)
            o_ref[pl.ds(r0, c), lanes[i]] = (_rms_scale(outs[i], nw_ref[...]) * _silu(z)).astype(o_ref.dtype)
        return carry

    lax.fori_loop(0, nchunk, phase_b, 0)


def gdn_core(proj, gates, gates_t, norm_w, side, *, batch, seq, hb, ts, chunks_a):
    t = proj.shape[0]
    w = hb * LANES
    ngroups = GDN_HEADS // hb
    nt = seq // ts
    nchunk = ts // DELTA_CHUNK
    side_in, side_out, side_shapes, side_args = _side_specs(
        side, batch * ngroups * nt, lambda b, h, s: (b * ngroups + h) * nt + s)
    kern = functools.partial(_gdn_kernel, hb=hb, ts=ts, chunks_a=chunks_a,
                             side_groups=tuple(len(g) for g in side))

    def col_spec(base):
        return pl.BlockSpec((ts, w), lambda b, h, s: (b * nt + s, base * ngroups + h))

    return pl.pallas_call(
        kern,
        grid=(batch, ngroups, nt),
        in_specs=[
            col_spec(0), col_spec(1), col_spec(2), col_spec(3),
            pl.BlockSpec((ts, LANES), lambda b, h, s: (b * nt + s, 0)),
            pl.BlockSpec((nchunk, LANES, DELTA_CHUNK), lambda b, h, s: (b * nt + s, 0, 0)),
            pl.BlockSpec((1, LANES), lambda b, h, s: (0, 0)),
        ] + side_in,
        out_specs=[pl.BlockSpec((ts, w), lambda b, h, s: (b * nt + s, h))] + side_out,
        out_shape=[jax.ShapeDtypeStruct((t, GDN_V_DIM), BF16)] + side_shapes,
        scratch_shapes=[
            pltpu.VMEM((hb, GDN_DK, GDN_DV), F32),
            pltpu.VMEM((ts, w), F32),
            pltpu.VMEM((2, ts, w), BF16),
            pltpu.VMEM((ts, w), BF16),
            pltpu.VMEM((ts, w), BF16),
            pltpu.VMEM((nchunk, hb, SUBLANES, LANES), F32),
        ],
        compiler_params=_params("parallel", "parallel", "arbitrary"),
        name="gdn_core",
    )(proj, proj, proj, proj, gates, gates_t, norm_w.reshape(1, LANES), *side_args)


def _attn_kernel(qi_tab, ki_tab, *refs, tq, tk, lambda_init, side_groups):
    n_in, n_side = 5, sum(side_groups)
    q_ref, k_ref, v_ref, lam_ref, sw_ref = refs[:n_in]
    side_in = refs[n_in:n_in + n_side]
    o_ref = refs[n_in + n_side]
    side_out = refs[n_in + n_side + 1:n_in + n_side + 1 + len(side_groups)]
    m_ref, l_ref, acc_ref = refs[n_in + n_side + 1 + len(side_groups):]
    p = pl.program_id(2)
    qi = qi_tab[p]
    ki = ki_tab[p]
    dh = DIFF_DH

    pl.when(p == 0)(functools.partial(_cast_side, side_in, side_out, side_groups))

    @pl.when(ki == 0)
    def _():
        m_ref[...] = jnp.full_like(m_ref, -jnp.inf)
        l_ref[...] = jnp.zeros_like(l_ref)
        acc_ref[...] = jnp.zeros_like(acc_ref)

    parts = tq // tk
    row = lax.broadcasted_iota(jnp.int32, (tk, tk), 0)
    col = lax.broadcasted_iota(jnp.int32, (tk, tk), 1)
    diag_visible = (col // CHUNK) <= (row // CHUNK)

    def update(diag_part):
        k = k_ref[...]
        v = v_ref[...]
        first = 0 if diag_part is None else diag_part
        chains = [(r, m) for r in range(first, parts) for m in range(2)]

        def scores(r, m):
            return lax.dot_general(q_ref[r * tk:(r + 1) * tk, m * dh:(m + 1) * dh], k[:, m * dh:(m + 1) * dh],
                                   NT_DIMS, preferred_element_type=F32)

        s_next = scores(*chains[0])
        for idx, (r, m) in enumerate(chains):
            s = s_next
            if idx + 1 < len(chains):
                s_next = scores(*chains[idx + 1])
            rows = slice(r * tk, (r + 1) * tk)
            if r == diag_part:
                s = jnp.where(diag_visible, s, -jnp.inf)
            m_old = m_ref[m, rows, :]
            m_new = jnp.maximum(m_old, jnp.max(s, axis=-1, keepdims=True))
            alpha = jnp.exp2(m_old - m_new)
            pr = jnp.exp2(s - jnp.concatenate([m_new] * (tk // LANES), axis=1))
            psum = pr[:, 0:LANES]
            for t in range(1, tk // LANES):
                psum = psum + pr[:, t * LANES:(t + 1) * LANES]
            l_ref[m, rows, :] = alpha * l_ref[m, rows, :] + psum
            acc_ref[m, rows, :] = (jnp.concatenate([alpha] * (2 * dh // LANES), axis=1) * acc_ref[m, rows, :]
                                   + jnp.dot(pr.astype(BF16), v, preferred_element_type=F32))
            m_ref[m, rows, :] = m_new

    d = ki - qi * parts
    pl.when(d < 0)(functools.partial(update, None))
    for j in range(parts):
        pl.when(d == j)(functools.partial(update, j))

    @pl.when(d == parts - 1)
    def _():
        lam = (jnp.exp(jnp.sum(lam_ref[0:1, :] * lam_ref[1:2, :], axis=-1, keepdims=True))
               - jnp.exp(jnp.sum(lam_ref[2:3, :] * lam_ref[3:4, :], axis=-1, keepdims=True))
               + lambda_init)
        inv_l0 = 1.0 / jnp.sum(l_ref[0], axis=-1, keepdims=True)
        inv_l1 = lam / jnp.sum(l_ref[1], axis=-1, keepdims=True)
        a = acc_ref[0] * inv_l0 - acc_ref[1] * inv_l1
        o_ref[...] = (_rms_scale(a, sw_ref[...]) * (1.0 - lambda_init)).astype(o_ref.dtype)


def diff_attention(q, kv, lam_rows, subln_w, side, *, batch, seq, tq, tk, lambda_init, q_col0):
    t = q.shape[0]
    hw = 2 * DIFF_DH
    nq = seq // tq
    nk = seq // tk
    side_in, side_out, side_shapes, side_args = _side_specs(
        side, batch * DIFF_HEADS, lambda b, h, p, qt, kt: b * DIFF_HEADS + h)
    pairs = [(a, b) for a in range(nq) for b in range(nk) if b * tk < (a + 1) * tq]
    qi_tab = jnp.asarray([a for a, _ in pairs], jnp.int32)
    ki_tab = jnp.asarray([b for _, b in pairs], jnp.int32)
    kern = functools.partial(_attn_kernel, tq=tq, tk=tk, lambda_init=lambda_init,
                             side_groups=tuple(len(g) for g in side))
    grid_spec = pltpu.PrefetchScalarGridSpec(
        num_scalar_prefetch=2,
        grid=(batch, DIFF_HEADS, len(pairs)),
        in_specs=[
            pl.BlockSpec((tq, hw), lambda b, h, p, qt, kt: (b * nq + qt[p], q_col0 + h)),
            pl.BlockSpec((tk, hw), lambda b, h, p, qt, kt: (b * nk + kt[p], h)),
            pl.BlockSpec((tk, hw), lambda b, h, p, qt, kt: (b * nk + kt[p], DIFF_HEADS + h)),
            pl.BlockSpec((SUBLANES, DIFF_DH), lambda b, h, p, qt, kt: (0, 0)),
            pl.BlockSpec((1, hw), lambda b, h, p, qt, kt: (0, 0)),
        ] + side_in,
        out_specs=[pl.BlockSpec((tq, hw), lambda b, h, p, qt, kt: (b * nq + qt[p], h))] + side_out,
        scratch_shapes=[
            pltpu.VMEM((2, tq, LANES), F32),
            pltpu.VMEM((2, tq, LANES), F32),
            pltpu.VMEM((2, tq, hw), F32),
        ],
    )
    return pl.pallas_call(
        kern,
        grid_spec=grid_spec,
        out_shape=[jax.ShapeDtypeStruct((t, DIFF_HEADS * hw), BF16)] + side_shapes,
        compiler_params=_params("parallel", "parallel", "arbitrary"),
        name="diff_attention",
    )(qi_tab, ki_tab, q, kv, kv, lam_rows, subln_w.reshape(1, hw), *side_args)


def kernel(x, norm_mix_pre, norm_mix_post, norm_ffn_pre, norm_ffn_post, gdn_w_in, gdn_conv_w, gdn_a_log,
           gdn_dt_bias, gdn_norm_w, gdn_w_out, kv_norm_w, w_kv, diff_w_q, diff_lq1, diff_lk1, diff_lq2,
           diff_lk2, diff_subln_w, diff_w_o, ffn_w_up, ffn_conv_w, ffn_conv_b, ffn_w_down):
    batch, seq, d = x.shape
    t = batch * seq
    h = x.reshape(t, d)
    n_main = 2 * GDN_QK_DIM + 2 * GDN_V_DIM

    for layer in range(DEPTH):
        if layer < N_A:
            w_in = gdn_w_in[layer].astype(BF16)
            w_ba = jnp.pad(w_in[:, n_main:], ((0, 0), (0, LANES - 2 * GDN_HEADS)))
            proj, ba = gdn_in_proj(h, norm_mix_pre[layer], w_in, w_ba, gdn_conv_w[layer], seq=seq, tm=1024,
                                   tn=1024, row_parts=4)
            prm = jnp.zeros((SUBLANES, LANES), F32)
            prm = prm.at[0, GDN_HEADS:2 * GDN_HEADS].set(gdn_a_log[layer].astype(F32))
            prm = prm.at[1, GDN_HEADS:2 * GDN_HEADS].set(gdn_dt_bias[layer].astype(F32))
            gates, gates_t = gdn_gates(ba, prm, chunks_per_step=4)
            side = [[(ffn_w_up, layer)], [(ffn_w_down, layer)]]
            o, w_up_b, w_down_b = gdn_core(proj, gates, gates_t, gdn_norm_w[layer], side, batch=batch, seq=seq,
                                           hb=8, ts=1024, chunks_a=2)
            side = [[(w_kv, None), (diff_w_q, 0)]] if layer == N_A - 1 else []
            h, *rest = matmul_norm_res(o, gdn_w_out[layer].astype(BF16), h, norm_mix_post[layer], tm=512,
                                       side=side)
            if rest:
                w_kvq_b = rest[0]
        else:
            j = layer - N_A
            lambda_init = 0.8 - 0.6 * math.exp(-0.3 * layer)
            q_scale = DIFF_DH ** -0.5 * math.log2(math.e)
            tn = 512
            if layer == N_A:
                kv = norm_matmul(h, jnp.stack([kv_norm_w, norm_mix_pre[layer]]), w_kvq_b, BF16, tm=1024, tn=tn,
                                 split=w_kv.shape[1] // tn, out_scale=q_scale)
                q, q_col0 = kv, w_kv.shape[1] // (2 * DIFF_DH)
            else:
                q = norm_matmul(h, norm_mix_pre[layer][None], diff_w_q[j].astype(BF16), BF16, tm=1024, tn=tn,
                                out_scale=q_scale)
                q_col0 = 0
            lam_rows = jnp.zeros((SUBLANES, DIFF_DH), F32)
            lam_rows = lam_rows.at[0].set(diff_lq1[j]).at[1].set(diff_lk1[j])
            lam_rows = lam_rows.at[2].set(diff_lq2[j]).at[3].set(diff_lk2[j])
            side = [[(ffn_w_up, layer)], [(ffn_w_down, layer)], [(diff_w_o, j)]]
            att, w_up_b, w_down_b, w_o_b = diff_attention(q, kv, lam_rows, diff_subln_w[j], side, batch=batch,
                                                          seq=seq, tq=2048, tk=512, lambda_init=lambda_init,
                                                          q_col0=q_col0)
            h, = matmul_norm_res(att, w_o_b, h, norm_mix_post[layer], tm=512)
        h = conv_ffn(h, norm_ffn_pre[layer], w_up_b, ffn_conv_w[layer], ffn_conv_b[layer], w_down_b,
                     norm_ffn_post[layer], seq=seq, tm=512, tf=512)
    return h.reshape(batch, seq, d)
```
